```python
import math
import jax, jax.numpy as jnp
from jax import lax
import numpy as np

D_MODEL = 2048
BATCH = 4
SEQ = 2048
DEPTH = 2

N_MIXERS = 2
N_SB_LAYERS = (DEPTH + 1) // 2
N_GDN_LAYERS = DEPTH // 2

SB_HEADS = 16
SB_HEAD_DIM = D_MODEL // SB_HEADS
SB_WIDTH = SB_HEADS * SB_HEAD_DIM
SB_BLOCK = 128

GDN_K_HEADS = 16
GDN_V_HEADS = 32
GDN_HEAD_DIM = D_MODEL // 16
GDN_KEY_DIM = GDN_K_HEADS * GDN_HEAD_DIM
GDN_VALUE_DIM = GDN_V_HEADS * GDN_HEAD_DIM
GDN_CONV = 4
GDN_CONV_CH = 2 * GDN_KEY_DIM + GDN_VALUE_DIM
GDN_IN = GDN_CONV_CH + GDN_VALUE_DIM + 2 * GDN_V_HEADS
GDN_CHUNK = 64

N_EXPERTS = 32
TOP_K = 4
EXPERT_FF = D_MODEL
SWIGLU_LIMIT = 7.0
SWIGLU_ALPHA = 1.702
MOE_BLOCK = 128

DEEPNORM_ALPHA = (2 * DEPTH) ** 0.25
DEEPNORM_BETA = (8 * DEPTH) ** -0.25

LN_EPS = 1e-5
RMS_EPS = 1e-6
L2_EPS = 1e-6

kernel_name = 'hybrid_stickbreak_gdn_moe_deepnorm'


def layer_norm(x, g, b):
    xf = x.astype(jnp.float32)
    mu = jnp.mean(xf, axis=-1, keepdims=True)
    xc = xf - mu
    var = jnp.mean(xc * xc, axis=-1, keepdims=True)
    return (xc * lax.rsqrt(var + LN_EPS) * g.astype(jnp.float32) + b.astype(jnp.float32)).astype(x.dtype)


def l2norm(x):
    xf = x.astype(jnp.float32)
    return xf * lax.rsqrt(jnp.sum(xf * xf, axis=-1, keepdims=True) + L2_EPS)


def stick_breaking_attention(h, w_qkv, w_o):
    B, T, _ = h.shape
    qkv = (h @ w_qkv).reshape(B, T, 3, SB_HEADS, SB_HEAD_DIM)
    q = jnp.moveaxis(qkv[:, :, 0], 2, 1).astype(jnp.float32) * (SB_HEAD_DIM ** -0.5)
    k = jnp.moveaxis(qkv[:, :, 1], 2, 1).astype(jnp.float32)
    v = jnp.moveaxis(qkv[:, :, 2], 2, 1)
    outs = []
    for blk in range(T // SB_BLOCK):
        q0 = blk * SB_BLOCK
        n_keys = q0 + SB_BLOCK
        z = jnp.einsum('bhqd,bhkd->bhqk', q[:, :, q0:n_keys], k[:, :, :n_keys])
        causal = jnp.arange(n_keys)[None, :] < (q0 + jnp.arange(SB_BLOCK))[:, None]
        log_1m_beta = jnp.where(causal, jax.nn.log_sigmoid(-z), 0.0)
        log_stick = lax.cumsum(log_1m_beta, axis=3, reverse=True) - log_1m_beta
        weights = jnp.where(causal, jnp.exp(jax.nn.log_sigmoid(z) + log_stick), 0.0)
        outs.append(jnp.einsum('bhqk,bhkd->bhqd', weights.astype(v.dtype), v[:, :, :n_keys]))
    o = jnp.concatenate(outs, axis=2)
    return jnp.moveaxis(o, 1, 2).reshape(B, T, SB_WIDTH) @ w_o


def causal_depthwise_conv(x, w):
    kw, ch = w.shape
    return lax.conv_general_dilated(
        x, w[:, None, :].astype(x.dtype), window_strides=(1,), padding=[(kw - 1, 0)],
        dimension_numbers=('NWC', 'WIO', 'NWC'), feature_group_count=ch)


def chunk_gated_delta_rule(q, k, v, g, beta):
    B, T, H, DK = k.shape
    DV = v.shape[-1]
    C = GDN_CHUNK
    N = T // C

    def chunks(t):
        t = t.astype(jnp.float32).reshape((B, N, C, H) + t.shape[3:])
        return jnp.moveaxis(t, 3, 1)

    q_c, k_c, v_c = chunks(q), chunks(k), chunks(v)
    g_c = lax.cumsum(chunks(g), axis=3)
    b_c = chunks(beta)
    incl = jnp.tril(jnp.ones((C, C), dtype=bool))
    strict = jnp.tril(jnp.ones((C, C), dtype=bool), -1)
    diff = g_c[..., :, None] - g_c[..., None, :]
    decay = jnp.where(incl, jnp.exp(jnp.where(incl, diff, 0.0)), 0.0)
    k_beta = k_c * b_c[..., None]
    lower = jnp.where(strict, jnp.einsum('bhnid,bhnjd->bhnij', k_beta, k_c) * decay, 0.0)
    unit_lower = lower + jnp.eye(C, dtype=jnp.float32)
    u = lax.linalg.triangular_solve(unit_lower, v_c * b_c[..., None], left_side=True, lower=True, unit_diagonal=True)
    w = lax.linalg.triangular_solve(unit_lower, k_beta * jnp.exp(g_c)[..., None], left_side=True, lower=True, unit_diagonal=True)
    qk = jnp.where(incl, jnp.einsum('bhnid,bhnjd->bhnij', q_c, k_c) * decay, 0.0)
    xs = tuple(jnp.moveaxis(t, 2, 0) for t in (q_c, k_c, u, w, g_c, qk))

    def step(state, inp):
        q_i, k_i, u_i, w_i, g_i, qk_i = inp
        v_new = u_i - jnp.einsum('bhcd,bhde->bhce', w_i, state)
        o_i = (jnp.einsum('bhcd,bhde->bhce', q_i * jnp.exp(g_i)[..., None], state)
               + jnp.einsum('bhij,bhje->bhie', qk_i, v_new))
        g_last = g_i[..., -1:]
        state = (state * jnp.exp(g_last)[..., None]
                 + jnp.einsum('bhcd,bhce->bhde', k_i * jnp.exp(g_last - g_i)[..., None], v_new))
        return state, o_i

    state0 = jnp.zeros((B, H, DK, DV), jnp.float32)
    _, o = lax.scan(step, state0, xs)
    o = jnp.moveaxis(o, 0, 2)
    return jnp.moveaxis(o, 1, 3).reshape(B, T, H, DV)


def gated_deltanet(h, w_in, conv_w, a_log, dt_bias, norm_w, w_o):
    B, T, _ = h.shape
    proj = h @ w_in
    s1 = GDN_CONV_CH
    s2 = s1 + GDN_VALUE_DIM
    s3 = s2 + GDN_V_HEADS
    qkv, z, b, a = proj[..., :s1], proj[..., s1:s2], proj[..., s2:s3], proj[..., s3:]
    qkv = jax.nn.silu(causal_depthwise_conv(qkv, conv_w))
    q = qkv[..., :GDN_KEY_DIM].reshape(B, T, GDN_K_HEADS, GDN_HEAD_DIM)
    k = qkv[..., GDN_KEY_DIM:2 * GDN_KEY_DIM].reshape(B, T, GDN_K_HEADS, GDN_HEAD_DIM)
    v = qkv[..., 2 * GDN_KEY_DIM:].reshape(B, T, GDN_V_HEADS, GDN_HEAD_DIM)
    rep = GDN_V_HEADS // GDN_K_HEADS
    q = jnp.repeat(l2norm(q) * (GDN_HEAD_DIM ** -0.5), rep, axis=2)
    k = jnp.repeat(l2norm(k), rep, axis=2)
    beta = jax.nn.sigmoid(b.astype(jnp.float32))
    g = -jnp.exp(a_log.astype(jnp.float32)) * jax.nn.softplus(a.astype(jnp.float32) + dt_bias.astype(jnp.float32))
    o = chunk_gated_delta_rule(q, k, v, g, beta)
    zf = z.reshape(B, T, GDN_V_HEADS, GDN_HEAD_DIM).astype(jnp.float32)
    o = o * lax.rsqrt(jnp.mean(o * o, axis=-1, keepdims=True) + RMS_EPS) * norm_w.astype(jnp.float32) * jax.nn.silu(zf)
    return o.reshape(B, T, GDN_VALUE_DIM).astype(h.dtype) @ w_o


def moe_ffn(h, router_w, router_b, w_gu, b_gu, w_down, b_down):
    Bb, T, D = h.shape
    x2 = h.reshape(-1, D)
    n_tok = x2.shape[0]
    logits = (x2 @ router_w).astype(jnp.float32) + router_b.astype(jnp.float32)
    top_logit, top_e = lax.top_k(logits, TOP_K)
    gate = jax.nn.softmax(top_logit, axis=-1)
    n_slot = n_tok * TOP_K
    flat_e = top_e.reshape(-1)
    flat_tok = jnp.repeat(jnp.arange(n_tok, dtype=jnp.int32), TOP_K)
    flat_gate = gate.reshape(-1)
    order = jnp.argsort(flat_e)
    e_sorted = flat_e[order]
    counts = jnp.bincount(flat_e, length=N_EXPERTS)
    padded = (counts + MOE_BLOCK - 1) // MOE_BLOCK * MOE_BLOCK
    pad_end = jnp.cumsum(padded)
    pad_start = pad_end - padded
    start = jnp.cumsum(counts) - counts
    dest = pad_start[e_sorted] + jnp.arange(n_slot) - start[e_sorted]
    n_blocks = -(-n_slot // MOE_BLOCK) + N_EXPERTS
    n_rows = n_blocks * MOE_BLOCK
    row_tok = jnp.full((n_rows,), n_tok, dtype=jnp.int32).at[dest].set(flat_tok[order])
    row_gate = jnp.zeros((n_rows,), jnp.float32).at[dest].set(flat_gate[order])
    block_e = jnp.minimum(jnp.searchsorted(pad_end, jnp.arange(n_blocks) * MOE_BLOCK, side='right'), N_EXPERTS - 1)
    x_pad = jnp.concatenate([x2, jnp.zeros((1, D), x2.dtype)], axis=0)
    xb = x_pad[row_tok].reshape(n_blocks, MOE_BLOCK, D)

    def expert_block(args):
        xblk, e = args
        gu = xblk @ w_gu[e] + b_gu[e]
        glu = jnp.minimum(gu[:, 0::2], SWIGLU_LIMIT)
        lin = jnp.clip(gu[:, 1::2], -SWIGLU_LIMIT, SWIGLU_LIMIT)
        act = glu * jax.nn.sigmoid(SWIGLU_ALPHA * glu) * (lin + 1.0)
        return act @ w_down[e] + b_down[e]

    yb = lax.map(expert_block, (xb, block_e))
    y = jnp.zeros((n_tok + 1, D), jnp.float32).at[row_tok].add(
        yb.reshape(n_rows, D).astype(jnp.float32) * row_gate[:, None])
    return y[:n_tok].astype(h.dtype).reshape(Bb, T, D)


def setup_inputs(seed: int = 0) -> dict:
    key = jax.random.key(seed)
    ks = jax.random.split(key, 22)
    f32 = jnp.float32
    din = D_MODEL ** -0.5

    def nrm(k, shape, scale):
        return jax.random.normal(k, shape, f32) * scale

    x = nrm(ks[0], (BATCH, SEQ, D_MODEL), 1.0)
    sb_w_qkv = jnp.concatenate([
        nrm(ks[1], (N_SB_LAYERS, D_MODEL, 2 * SB_WIDTH), din),
        nrm(ks[2], (N_SB_LAYERS, D_MODEL, SB_WIDTH), din * DEEPNORM_BETA)], axis=-1)
    sb_w_o = nrm(ks[3], (N_SB_LAYERS, SB_WIDTH, D_MODEL), SB_WIDTH ** -0.5 * DEEPNORM_BETA)
    gdn_w_in = jnp.concatenate([
        nrm(ks[4], (N_GDN_LAYERS, D_MODEL, 2 * GDN_KEY_DIM), din),
        nrm(ks[5], (N_GDN_LAYERS, D_MODEL, GDN_VALUE_DIM), din * DEEPNORM_BETA),
        nrm(ks[6], (N_GDN_LAYERS, D_MODEL, GDN_VALUE_DIM + 2 * GDN_V_HEADS), din)], axis=-1)
    gdn_conv_w = nrm(ks[7], (N_GDN_LAYERS, GDN_CONV, GDN_CONV_CH), GDN_CONV ** -0.5)
    gdn_a_log = jnp.log(jax.random.uniform(ks[8], (N_GDN_LAYERS, GDN_V_HEADS), f32, 1.0, 16.0))
    dt = jnp.exp(jax.random.uniform(ks[9], (N_GDN_LAYERS, GDN_V_HEADS), f32, math.log(1e-3), math.log(1e-1)))
    gdn_dt_bias = dt + jnp.log(-jnp.expm1(-dt))
    gdn_norm_w = 1.0 + nrm(ks[10], (N_GDN_LAYERS, GDN_HEAD_DIM), 0.02)
    gdn_w_o = nrm(ks[11], (N_GDN_LAYERS, GDN_VALUE_DIM, D_MODEL), GDN_VALUE_DIM ** -0.5 * DEEPNORM_BETA)
    ln_mix_g = 1.0 + nrm(ks[12], (DEPTH, D_MODEL), 0.02)
    ln_mix_b = nrm(ks[13], (DEPTH, D_MODEL), 0.01)
    ln_ffn_g = 1.0 + nrm(ks[14], (DEPTH, D_MODEL), 0.02)
    ln_ffn_b = nrm(ks[15], (DEPTH, D_MODEL), 0.01)
    moe_router_w = nrm(ks[16], (DEPTH, D_MODEL, N_EXPERTS), din)
    moe_router_b = nrm(ks[17], (DEPTH, N_EXPERTS), 0.01)
    moe_w_gu = nrm(ks[18], (DEPTH, N_EXPERTS, D_MODEL, 2 * EXPERT_FF), din)
    moe_b_gu = nrm(ks[19], (DEPTH, N_EXPERTS, 2 * EXPERT_FF), 0.01)
    moe_w_down = nrm(ks[20], (DEPTH, N_EXPERTS, EXPERT_FF, D_MODEL), EXPERT_FF ** -0.5 * DEEPNORM_BETA)
    moe_b_down = nrm(ks[21], (DEPTH, N_EXPERTS, D_MODEL), 0.01)
    return {'x': x, 'sb_w_qkv': sb_w_qkv, 'sb_w_o': sb_w_o,
            'gdn_w_in': gdn_w_in, 'gdn_conv_w': gdn_conv_w, 'gdn_a_log': gdn_a_log,
            'gdn_dt_bias': gdn_dt_bias, 'gdn_norm_w': gdn_norm_w, 'gdn_w_o': gdn_w_o,
            'ln_mix_g': ln_mix_g, 'ln_mix_b': ln_mix_b, 'ln_ffn_g': ln_ffn_g, 'ln_ffn_b': ln_ffn_b,
            'moe_router_w': moe_router_w, 'moe_router_b': moe_router_b,
            'moe_w_gu': moe_w_gu, 'moe_b_gu': moe_b_gu, 'moe_w_down': moe_w_down, 'moe_b_down': moe_b_down}


def reference(x, sb_w_qkv, sb_w_o, gdn_w_in, gdn_conv_w, gdn_a_log, gdn_dt_bias, gdn_norm_w, gdn_w_o,
              ln_mix_g, ln_mix_b, ln_ffn_g, ln_ffn_b, moe_router_w, moe_router_b,
              moe_w_gu, moe_b_gu, moe_w_down, moe_b_down):
    for i in range(DEPTH):
        j = i // N_MIXERS
        if i % N_MIXERS == 0:
            mix = stick_breaking_attention(x, sb_w_qkv[j], sb_w_o[j])
        else:
            mix = gated_deltanet(x, gdn_w_in[j], gdn_conv_w[j], gdn_a_log[j], gdn_dt_bias[j],
                                 gdn_norm_w[j], gdn_w_o[j])
        x = layer_norm(DEEPNORM_ALPHA * x + mix, ln_mix_g[i], ln_mix_b[i])
        ffn = moe_ffn(x, moe_router_w[i], moe_router_b[i], moe_w_gu[i], moe_b_gu[i], moe_w_down[i], moe_b_down[i])
        x = layer_norm(DEEPNORM_ALPHA * x + ffn, ln_ffn_g[i], ln_ffn_b[i])
    return x
```

```python
import functools

import jax
import jax.numpy as jnp
from jax import lax
from jax.experimental import pallas as pl
from jax.experimental.pallas import tpu as pltpu

F32, BF16, I32 = jnp.float32, jnp.bfloat16, jnp.int32

DEPTH = 2
N_EXPERTS = 32
TOP_K = 4
HEAD_DIM = 128
SB_HEADS = 16
GDN_K_HEADS = 16
GDN_V_HEADS = 32
GDN_CONV = 4
GDN_CHUNK = 64
SWIGLU_LIMIT = 7.0
SWIGLU_ALPHA = 1.702
DEEPNORM_ALPHA = (2 * DEPTH) ** 0.25
LN_EPS = 1e-5
RMS_EPS = 1e-6
L2_EPS = 1e-6

VMEM_LIMIT_BYTES = 56 * 1024 * 1024
MOE_ROWS = 256
ATTN_BLOCK = 256


def _params(n_axes):
    return pltpu.CompilerParams(dimension_semantics=("arbitrary",) * n_axes,
                                vmem_limit_bytes=VMEM_LIMIT_BYTES)


def _iota(shape, dim):
    return lax.broadcasted_iota(I32, shape, dim)


def _split_bf16(x):
    hi = x.astype(BF16)
    lo = (x - hi.astype(F32)).astype(BF16)
    return hi, lo


def _dot(a, b):
    return jnp.dot(a, b, preferred_element_type=F32)


def _dot_nt(a, b):
    return lax.dot_general(a, b, (((1,), (1,)), ((), ())), preferred_element_type=F32)


def _mm_kernel(a_ref, w_ref, o_ref):
    o_ref[...] = _dot(a_ref[...], w_ref[...]).astype(o_ref.dtype)


def _dense_matmul(a, w, *, col0=0, n_cols=None, out_dtype=F32, tm=512, tn=1024):
    m_rows, k_dim = a.shape
    n_cols = w.shape[1] if n_cols is None else n_cols
    tn = min(tn, n_cols)
    tm = min(tm, m_rows)
    assert n_cols % tn == 0 and col0 % tn == 0 and m_rows % tm == 0
    off = col0 // tn
    return pl.pallas_call(
        _mm_kernel,
        grid=(n_cols // tn, m_rows // tm),
        in_specs=[pl.BlockSpec((tm, k_dim), lambda n, m: (m, 0)),
                  pl.BlockSpec((k_dim, tn), lambda n, m: (0, n + off))],
        out_specs=pl.BlockSpec((tm, tn), lambda n, m: (m, n)),
        out_shape=jax.ShapeDtypeStruct((m_rows, n_cols), out_dtype),
        compiler_params=_params(2),
    )(a, w)


def _sb_attn_kernel(q_ref, k_ref, v_ref, o_ref, *, blk, scale):
    i = pl.program_id(2)
    q = q_ref[...]
    rows = _iota((blk, blk), 0)
    cols = _iota((blk, blk), 1)
    later = (rows > cols).astype(BF16)
    causal = cols < rows

    def visit(kb, vb, carry, acc, diagonal):
        z = _dot_nt(q, kb) * scale
        log_beta = jnp.minimum(z, 0.0) - jnp.log(1.0 + jnp.exp(-jnp.abs(z)))
        log_1m = log_beta - z
        if diagonal:
            log_1m = jnp.where(causal, log_1m, 0.0)
        hi, lo = _split_bf16(log_1m)
        after = _dot(hi, later) + _dot(lo, later)
        w = jnp.exp(log_beta + after + carry)
        if diagonal:
            w = jnp.where(causal, w, 0.0)
        acc = acc + _dot(w.astype(BF16), vb)
        carry = carry + jnp.sum(log_1m, axis=1, keepdims=True)
        return carry, acc

    d0 = pl.multiple_of(i * blk, blk)
    carry, acc = visit(k_ref[pl.ds(d0, blk), :], v_ref[pl.ds(d0, blk), :],
                       jnp.zeros((blk, 1), F32), jnp.zeros((blk, HEAD_DIM), F32), True)

    def body(t, state):
        j0 = pl.multiple_of((i - 1 - t) * blk, blk)
        return visit(k_ref[pl.ds(j0, blk), :], v_ref[pl.ds(j0, blk), :], *state, False)

    carry, acc = lax.fori_loop(0, i, body, (carry, acc))
    o_ref[...] = acc.astype(o_ref.dtype)


def _sb_attention(qkv, batch, seq):
    blk = min(ATTN_BLOCK, seq)
    nq = seq // blk
    h = SB_HEADS
    kernel = functools.partial(_sb_attn_kernel, blk=blk, scale=HEAD_DIM ** -0.5)
    return pl.pallas_call(
        kernel,
        grid=(batch, h, nq),
        in_specs=[pl.BlockSpec((blk, HEAD_DIM), lambda b, hh, i: (b * nq + i, hh)),
                  pl.BlockSpec((seq, HEAD_DIM), lambda b, hh, i: (b, h + hh)),
                  pl.BlockSpec((seq, HEAD_DIM), lambda b, hh, i: (b, 2 * h + hh))],
        out_specs=pl.BlockSpec((blk, HEAD_DIM), lambda b, hh, i: (b * nq + i, hh)),
        out_shape=jax.ShapeDtypeStruct((batch * seq, h * HEAD_DIM), BF16),
        compiler_params=_params(3),
    )(qkv, qkv, qkv)


def _deepnorm_ln(x, upd, g, b):
    h = DEEPNORM_ALPHA * x + upd
    mu = jnp.mean(h, axis=-1, keepdims=True)
    hc = h - mu
    var = jnp.mean(hc * hc, axis=-1, keepdims=True)
    return hc * lax.rsqrt(var + LN_EPS) * g + b


def _ln_router_kernel(x_ref, u_ref, g_ref, b_ref, rw_ref, rb_ref,
                      xo_ref, e_ref, gate_ref, rank_ref, cnt_ref, base_ref, *, tm):
    i = pl.program_id(0)

    @pl.when(i == 0)
    def _():
        base_ref[...] = jnp.zeros_like(base_ref)

    y = _deepnorm_ln(x_ref[...], u_ref[...], g_ref[...], b_ref[...])
    xo_ref[...] = y

    yh, yl = _split_bf16(y)
    wh, wl = _split_bf16(rw_ref[...])
    logits = _dot(yh, wh) + _dot(yl, wh) + _dot(yh, wl) + rb_ref[...]

    lane = _iota((tm, N_EXPERTS), 1)
    slot = _iota((tm, TOP_K), 1)
    work = logits
    picks, tops = [], []
    chosen = jnp.zeros((tm, N_EXPERTS), F32)
    for _ in range(TOP_K):
        top = jnp.max(work, axis=1, keepdims=True)
        idx = jnp.min(jnp.where(work == top, lane, N_EXPERTS), axis=1, keepdims=True)
        hit = lane == idx
        chosen = jnp.where(hit, 1.0, chosen)
        work = jnp.where(hit, -jnp.inf, work)
        picks.append(idx)
        tops.append(top)

    exps = [jnp.exp(t - tops[0]) for t in tops]
    denom = exps[0] + exps[1] + exps[2] + exps[3]

    before = (_iota((tm, tm), 0) > _iota((tm, tm), 1)).astype(BF16)
    ahead = base_ref[...] + _dot(before, chosen.astype(BF16))

    e_out = jnp.zeros((tm, TOP_K), I32)
    g_out = jnp.zeros((tm, TOP_K), F32)
    r_out = jnp.zeros((tm, TOP_K), F32)
    for k in range(TOP_K):
        rank_k = jnp.sum(jnp.where(lane == picks[k], ahead, 0.0), axis=1, keepdims=True)
        e_out = jnp.where(slot == k, picks[k], e_out)
        g_out = jnp.where(slot == k, exps[k] / denom, g_out)
        r_out = jnp.where(slot == k, rank_k, r_out)
    e_ref[...] = e_out
    gate_ref[...] = g_out
    rank_ref[...] = r_out.astype(I32)

    base_ref[...] = base_ref[...] + jnp.sum(chosen, axis=0, keepdims=True)
    cnt_ref[...] = base_ref[...]


def _ln_router(x, upd, g, b, router_w, router_b, tm=256):
    m_rows, d = x.shape
    tm = min(tm, m_rows)
    row = lambda i: (i, 0)
    fixed = lambda i: (0, 0)
    return pl.pallas_call(
        functools.partial(_ln_router_kernel, tm=tm),
        grid=(m_rows // tm,),
        in_specs=[pl.BlockSpec((tm, d), row), pl.BlockSpec((tm, d), row),
                  pl.BlockSpec((1, d), fixed), pl.BlockSpec((1, d), fixed),
                  pl.BlockSpec((d, N_EXPERTS), fixed), pl.BlockSpec((1, N_EXPERTS), fixed)],
        out_specs=[pl.BlockSpec((tm, d), row), pl.BlockSpec((tm, TOP_K), row),
                   pl.BlockSpec((tm, TOP_K), row), pl.BlockSpec((tm, TOP_K), row),
                   pl.BlockSpec((1, N_EXPERTS), fixed)],
        out_shape=[jax.ShapeDtypeStruct((m_rows, d), F32),
                   jax.ShapeDtypeStruct((m_rows, TOP_K), I32),
                   jax.ShapeDtypeStruct((m_rows, TOP_K), F32),
                   jax.ShapeDtypeStruct((m_rows, TOP_K), I32),
                   jax.ShapeDtypeStruct((1, N_EXPERTS), F32)],
        scratch_shapes=[pltpu.VMEM((1, N_EXPERTS), F32)],
        compiler_params=_params(1),
    )(x, upd, g.reshape(1, d), b.reshape(1, d), router_w, router_b.reshape(1, N_EXPERTS))


def _row_copy(src_hbm, row, dst, sem):
    return pltpu.make_async_copy(src_hbm.at[pl.ds(row, 1), :], dst, sem)


def _dispatch_kernel(tok_ref, x_hbm, o_ref, buf, sem, *, rb):
    i = pl.program_id(0)
    nb = pl.num_programs(0)

    def copies(blk, slot, go):
        def body(r, c):
            cp = _row_copy(x_hbm, tok_ref[blk * rb + r], buf.at[slot, pl.ds(r, 1), :], sem.at[slot])
            cp.start() if go else cp.wait()
            return c
        lax.fori_loop(0, rb, body, 0, unroll=8)

    @pl.when(i == 0)
    def _():
        copies(0, 0, True)

    @pl.when(i + 1 < nb)
    def _():
        copies(i + 1, (i + 1) % 2, True)

    copies(i, i % 2, False)
    o_ref[...] = buf[i % 2].astype(BF16)


def _dispatch(x, row_tok, rb):
    m_rows, d = x.shape
    n_rows = row_tok.shape[0]
    return pl.pallas_call(
        functools.partial(_dispatch_kernel, rb=rb),
        grid_spec=pltpu.PrefetchScalarGridSpec(
            num_scalar_prefetch=1,
            grid=(n_rows // rb,),
            in_specs=[pl.BlockSpec(memory_space=pl.ANY)],
            out_specs=pl.BlockSpec((rb, d), lambda i, tok: (i, 0)),
            scratch_shapes=[pltpu.VMEM((2, rb, d), F32), pltpu.SemaphoreType.DMA((2,))]),
        out_shape=jax.ShapeDtypeStruct((n_rows, d), BF16),
        compiler_params=_params(1),
    )(row_tok, x)


def _block_state(be_ref, nact_ref):
    m = pl.program_id(1)
    active = m < nact_ref[0]
    fresh = jnp.logical_or(m == 0, be_ref[m] != be_ref[jnp.maximum(m - 1, 0)])
    return active, jnp.logical_and(active, fresh)


def _gate_up_kernel(be_ref, nact_ref, a_ref, w_ref, bg_ref, bl_ref, o_ref, wg_s, wl_s, *, tf):
    active, fresh = _block_state(be_ref, nact_ref)
    grp = 2 * HEAD_DIM

    @pl.when(fresh)
    def _():
        r = _iota((grp, grp), 0)
        c = _iota((grp, grp), 1)
        source = jnp.where(c < HEAD_DIM, 2 * c, 2 * (c - HEAD_DIM) + 1)
        pick = (r == source).astype(BF16)
        for g in range(2 * tf // grp):
            wb = w_ref[:, g * grp:(g + 1) * grp].astype(BF16)
            sel = _dot(wb, pick)
            wg_s[:, g * HEAD_DIM:(g + 1) * HEAD_DIM] = sel[:, :HEAD_DIM].astype(BF16)
            wl_s[:, g * HEAD_DIM:(g + 1) * HEAD_DIM] = sel[:, HEAD_DIM:].astype(BF16)

    @pl.when(active)
    def _():
        a = a_ref[...]
        glu = jnp.minimum(_dot(a, wg_s[...]) + bg_ref[0], SWIGLU_LIMIT)
        lin = jnp.clip(_dot(a, wl_s[...]) + bl_ref[0], -SWIGLU_LIMIT, SWIGLU_LIMIT)
        act = glu * jax.nn.sigmoid(SWIGLU_ALPHA * glu) * (lin + 1.0)
        o_ref[...] = act.astype(o_ref.dtype)

    @pl.when(jnp.logical_not(active))
    def _():
        o_ref[...] = jnp.zeros_like(o_ref)


def _down_kernel(be_ref, nact_ref, a_ref, w_ref, b_ref, o_ref, w_s):
    active, fresh = _block_state(be_ref, nact_ref)

    @pl.when(fresh)
    def _():
        w_s[...] = w_ref[...].astype(BF16)

    @pl.when(active)
    def _():
        o_ref[...] = (_dot(a_ref[...], w_s[...]) + b_ref[0]).astype(o_ref.dtype)

    @pl.when(jnp.logical_not(active))
    def _():
        o_ref[...] = jnp.zeros_like(o_ref)


def _grouped_specs(rb, k_dim, w_cols, b_cols, n_bias, layer):
    def m_eff(m, nact):
        return jnp.minimum(m, nact[0] - 1)
    a_spec = pl.BlockSpec((rb, k_dim), lambda n, m, be, nact: (m_eff(m, nact), 0))
    w_spec = pl.BlockSpec((None, None, k_dim, w_cols),
                          lambda n, m, be, nact: (layer, be[m_eff(m, nact)], 0, n))
    b_spec = pl.BlockSpec((1, 1, b_cols), lambda n, m, be, nact: (be[m_eff(m, nact)], 0, n))
    o_spec = pl.BlockSpec((rb, b_cols), lambda n, m, be, nact: (m, n))
    return [a_spec, w_spec] + [b_spec] * n_bias, o_spec


def _moe_gate_up(xs, w_gu, layer, b_glu, b_lin, block_e, n_active, rb, tf=512):
    n_rows, d = xs.shape
    ff = w_gu.shape[3] // 2
    in_specs, o_spec = _grouped_specs(rb, d, 2 * tf, tf, 2, layer)
    return pl.pallas_call(
        functools.partial(_gate_up_kernel, tf=tf),
        grid_spec=pltpu.PrefetchScalarGridSpec(
            num_scalar_prefetch=2, grid=(ff // tf, n_rows // rb),
            in_specs=in_specs, out_specs=o_spec,
            scratch_shapes=[pltpu.VMEM((d, tf), BF16), pltpu.VMEM((d, tf), BF16)]),
        out_shape=jax.ShapeDtypeStruct((n_rows, ff), BF16),
        compiler_params=_params(2),
    )(block_e, n_active, xs, w_gu, b_glu, b_lin)


def _moe_down(act, w_down, layer, b_down, block_e, n_active, rb, tn=1024):
    n_rows, ff = act.shape
    d = w_down.shape[3]
    in_specs, o_spec = _grouped_specs(rb, ff, tn, tn, 1, layer)
    return pl.pallas_call(
        _down_kernel,
        grid_spec=pltpu.PrefetchScalarGridSpec(
            num_scalar_prefetch=2, grid=(d // tn, n_rows // rb),
            in_specs=in_specs, out_specs=o_spec,
            scratch_shapes=[pltpu.VMEM((ff, tn), BF16)]),
        out_shape=jax.ShapeDtypeStruct((n_rows, d), F32),
        compiler_params=_params(2),
    )(block_e, n_active, act, w_down, b_down)


def _combine_kernel(dest_ref, y_hbm, x_ref, gate_ref, g_ref, b_ref, *rest, tm, with_bf16):
    if with_bf16:
        xo_ref, xb_ref, buf, sem = rest
    else:
        xo_ref, buf, sem = rest
    i = pl.program_id(0)
    nb = pl.num_programs(0)
    n_slots = tm * TOP_K

    def copies(blk, slot, go):
        def body(r, c):
            tok, k = r // TOP_K, r % TOP_K
            cp = _row_copy(y_hbm, dest_ref[blk * n_slots + r], buf.at[slot, k, pl.ds(tok, 1), :], sem.at[slot])
            cp.start() if go else cp.wait()
            return c
        lax.fori_loop(0, n_slots, body, 0, unroll=8)

    @pl.when(i == 0)
    def _():
        copies(0, 0, True)

    @pl.when(i + 1 < nb)
    def _():
        copies(i + 1, (i + 1) % 2, True)

    copies(i, i % 2, False)
    gate = gate_ref[...]
    cur = buf.at[i % 2]
    ffn = gate[:, 0:1] * cur[0]
    for k in range(1, TOP_K):
        ffn = ffn + gate[:, k:k + 1] * cur[k]
    y = _deepnorm_ln(x_ref[...], ffn, g_ref[...], b_ref[...])
    xo_ref[...] = y
    if with_bf16:
        xb_ref[...] = y.astype(BF16)


def _combine(yb, dest, x, gate, g, b, with_bf16, tm=128):
    m_rows, d = x.shape
    tm = min(tm, m_rows)
    row = lambda i, dst: (i, 0)
    fixed = lambda i, dst: (0, 0)
    out_specs = [pl.BlockSpec((tm, d), row)]
    out_shape = [jax.ShapeDtypeStruct((m_rows, d), F32)]
    if with_bf16:
        out_specs.append(pl.BlockSpec((tm, d), row))
        out_shape.append(jax.ShapeDtypeStruct((m_rows, d), BF16))
    return pl.pallas_call(
        functools.partial(_combine_kernel, tm=tm, with_bf16=with_bf16),
        grid_spec=pltpu.PrefetchScalarGridSpec(
            num_scalar_prefetch=1, grid=(m_rows // tm,),
            in_specs=[pl.BlockSpec(memory_space=pl.ANY), pl.BlockSpec((tm, d), row),
                      pl.BlockSpec((tm, TOP_K), row), pl.BlockSpec((1, d), fixed), pl.BlockSpec((1, d), fixed)],
            out_specs=out_specs,
            scratch_shapes=[pltpu.VMEM((2, TOP_K, tm, d), F32), pltpu.SemaphoreType.DMA((2,))]),
        out_shape=out_shape,
        compiler_params=_params(1),
    )(dest, yb, x, gate, g.reshape(1, d), b.reshape(1, d))


def _moe_layer(x1, e_idx, gate, rank, counts, w_gu, b_gu, w_down, b_down, layer, ln_g, ln_b, with_bf16):
    m_rows, d = x1.shape
    rb = MOE_ROWS
    n_slots = m_rows * TOP_K
    n_blocks = -(-n_slots // rb) + N_EXPERTS
    n_rows = n_blocks * rb
    cnt = counts.reshape(N_EXPERTS).astype(I32)
    blocks_e = (cnt + rb - 1) // rb
    blk_end = jnp.cumsum(blocks_e)
    row_start = (blk_end - blocks_e) * rb
    dest = (row_start[e_idx] + rank).reshape(n_slots)
    n_active = blk_end[-1:].astype(I32)
    block_e = jnp.minimum(jnp.searchsorted(blk_end, jnp.arange(n_blocks, dtype=I32), side='right'),
                          N_EXPERTS - 1).astype(I32)
    slot_tok = jnp.arange(n_slots, dtype=I32) // TOP_K
    row_tok = jnp.zeros((n_rows,), I32).at[dest].set(slot_tok)

    ff = w_down.shape[2]
    b_pairs = b_gu.reshape(N_EXPERTS, ff, 2)
    b_glu = b_pairs[:, :, 0].reshape(N_EXPERTS, 1, ff)
    b_lin = b_pairs[:, :, 1].reshape(N_EXPERTS, 1, ff)

    xs = _dispatch(x1, row_tok, rb)
    act = _moe_gate_up(xs, w_gu, layer, b_glu, b_lin, block_e, n_active, rb)
    yb = _moe_down(act, w_down, layer, b_down.reshape(N_EXPERTS, 1, d), block_e, n_active, rb)
    return _combine(yb, dest, x1, gate, ln_g, ln_b, with_bf16)


def _conv_kernel(x_ref, halo_ref, w_ref, o_ref, ext, *, tt, nt, n_key_blocks, heads_per_block):
    i = pl.program_id(0)
    c = pl.program_id(1)
    first = (i % nt) == 0
    x = x_ref[...]
    ext[0:8, :] = jnp.where(first, 0.0, halo_ref[...])
    ext[8:, :] = x
    w = w_ref[...]
    acc = x * w[GDN_CONV - 1:GDN_CONV, :]
    for j in range(GDN_CONV - 1):
        acc = acc + ext[pl.ds(8 - (GDN_CONV - 1) + j, tt), :] * w[j:j + 1, :]
    y = acc * jax.nn.sigmoid(acc)

    def normed(scale):
        for hh in range(heads_per_block):
            seg = y[:, hh * HEAD_DIM:(hh + 1) * HEAD_DIM]
            inv = lax.rsqrt(jnp.sum(seg * seg, axis=1, keepdims=True) + L2_EPS)
            o_ref[:, hh * HEAD_DIM:(hh + 1) * HEAD_DIM] = seg * (inv * scale)

    @pl.when(c < n_key_blocks)
    def _():
        normed(HEAD_DIM ** -0.5)

    @pl.when(jnp.logical_and(c >= n_key_blocks, c < 2 * n_key_blocks))
    def _():
        normed(1.0)

    @pl.when(c >= 2 * n_key_blocks)
    def _():
        o_ref[...] = y


def _gdn_conv(pre, conv_w, batch, seq, tt=256, cb=1024):
    m_rows, ch = pre.shape
    tt = min(tt, seq)
    nt = seq // tt
    key_dim = GDN_K_HEADS * HEAD_DIM
    kernel = functools.partial(_conv_kernel, tt=tt, nt=nt, n_key_blocks=key_dim // cb,
                               heads_per_block=cb // HEAD_DIM)
    return pl.pallas_call(
        kernel,
        grid=(m_rows // tt, ch // cb),
        in_specs=[pl.BlockSpec((tt, cb), lambda i, c: (i, c)),
                  pl.BlockSpec((8, cb), lambda i, c: (jnp.maximum(i * (tt // 8) - 1, 0), c)),
                  pl.BlockSpec((GDN_CONV, cb), lambda i, c: (0, c))],
        out_specs=pl.BlockSpec((tt, cb), lambda i, c: (i, c)),
        out_shape=jax.ShapeDtypeStruct((m_rows, ch), F32),
        scratch_shapes=[pltpu.VMEM((tt + 8, cb), F32)],
        compiler_params=_params(2),
    )(pre, pre, conv_w)


def _gates_kernel(ba_ref, alog_ref, dtb_ref, beta_ref, gc_ref, *, tm):
    ba = ba_ref[...]
    beta_ref[...] = jax.nn.sigmoid(ba[:, :GDN_V_HEADS])
    pre = ba[:, GDN_V_HEADS:] + dtb_ref[...]
    softplus = jnp.maximum(pre, 0.0) + jnp.log(1.0 + jnp.exp(-jnp.abs(pre)))
    g = -jnp.exp(alog_ref[...]) * softplus
    r = _iota((tm, tm), 0)
    c = _iota((tm, tm), 1)
    upto = jnp.logical_and(r >= c, r // GDN_CHUNK == c // GDN_CHUNK).astype(BF16)
    g1 = g.astype(BF16)
    rem = g - g1.astype(F32)
    g2 = rem.astype(BF16)
    g3 = (rem - g2.astype(F32)).astype(BF16)
    gc_ref[...] = _dot(upto, g1) + _dot(upto, g2) + _dot(upto, g3)


def _gdn_gates(ba, a_log, dt_bias, tm=256):
    m_rows = ba.shape[0]
    tm = min(tm, m_rows)
    hv = GDN_V_HEADS
    return pl.pallas_call(
        functools.partial(_gates_kernel, tm=tm),
        grid=(m_rows // tm,),
        in_specs=[pl.BlockSpec((tm, 2 * hv), lambda i: (i, 0)),
                  pl.BlockSpec((1, hv), lambda i: (0, 0)), pl.BlockSpec((1, hv), lambda i: (0, 0))],
        out_specs=[pl.BlockSpec((tm, hv), lambda i: (i, 0)), pl.BlockSpec((tm, hv), lambda i: (i, 0))],
        out_shape=[jax.ShapeDtypeStruct((m_rows, hv), F32), jax.ShapeDtypeStruct((m_rows, hv), F32)],
        compiler_params=_params(1),
    )(ba, a_log.reshape(1, hv), dt_bias.reshape(1, hv))


def _gdn_core_kernel(q_ref, k_ref, v_ref, z_ref, beta_ref, gc_ref, nw_ref, o_ref, *, seq):
    hk = pl.program_id(1)
    ch = GDN_CHUNK
    rep = GDN_V_HEADS // GDN_K_HEADS
    ri = _iota((ch, ch), 0)
    ci = _iota((ch, ch), 1)
    incl, strict, eye = ri >= ci, ri > ci, ri == ci
    head_lane = _iota((ch, GDN_V_HEADS), 1)
    nw = nw_ref[...]

    def chunk(n, states):
        r0 = pl.multiple_of(n * ch, ch)
        rows = pl.ds(r0, ch)
        q = q_ref[rows, :]
        k = k_ref[rows, :]
        beta_all = beta_ref[rows, :]
        gc_all = gc_ref[rows, :]
        bcol, gcol, decay, kbeta = [], [], [], []
        for j in range(rep):
            pick = head_lane == (rep * hk + j)
            bj = jnp.sum(jnp.where(pick, beta_all, 0.0), axis=1, keepdims=True)
            gj = jnp.sum(jnp.where(pick, gc_all, 0.0), axis=1, keepdims=True)
            grow = jnp.sum(jnp.where(eye, gj, 0.0), axis=0, keepdims=True)
            dj = jnp.where(incl, jnp.exp(jnp.where(incl, gj - grow, 0.0)), 0.0)
            bcol.append(bj)
            gcol.append(gj)
            decay.append(dj)
            kbeta.append(k * bj)
        k16 = k.astype(BF16)
        stacked = jnp.concatenate([kb.astype(BF16) for kb in kbeta] + [q.astype(BF16)], axis=0)
        prods = _dot_nt(stacked, k16)
        qk_raw = prods[rep * ch:, :]
        new_states = []
        for j in range(rep):
            dj, gj, bj = decay[j], gcol[j], bcol[j]
            neg = -jnp.where(strict, prods[j * ch:(j + 1) * ch, :] * dj, 0.0)
            power = _dot(neg.astype(BF16), neg.astype(BF16))
            pm = neg
            for step in range(5):
                p16 = power.astype(BF16)
                if step < 4:
                    both = _dot(jnp.concatenate([p16, pm.astype(BF16)], axis=0), p16)
                    pm = pm + power + both[ch:, :]
                    power = both[:ch, :]
                else:
                    pm = pm + power + _dot(pm.astype(BF16), p16)
            eg = jnp.exp(gj)
            v = v_ref[rows, j * HEAD_DIM:(j + 1) * HEAD_DIM]
            rhs = jnp.concatenate([v * bj, kbeta[j] * eg], axis=1)
            uw = rhs + _dot(pm.astype(BF16), rhs.astype(BF16))
            u, w = uw[:, :HEAD_DIM], uw[:, HEAD_DIM:]
            state = states[j]
            s16 = state.astype(BF16)
            ws_qs = _dot(jnp.concatenate([w.astype(BF16), (q * eg).astype(BF16)], axis=0), s16)
            v_new = u - ws_qs[:ch, :]
            g_last = gj[ch - 1:ch, :]
            k_dec = k * jnp.exp(g_last - gj)
            qk = jnp.where(incl, qk_raw * dj, 0.0)
            lhs = jnp.concatenate([qk.astype(BF16), k_dec.T.astype(BF16)], axis=0)
            ov = _dot(lhs, v_new.astype(BF16))
            o = ws_qs[ch:, :] + ov[:ch, :]
            new_states.append(state * jnp.exp(g_last) + ov[ch:, :])
            z = z_ref[rows, j * HEAD_DIM:(j + 1) * HEAD_DIM]
            inv = lax.rsqrt(jnp.mean(o * o, axis=1, keepdims=True) + RMS_EPS)
            o_ref[rows, j * HEAD_DIM:(j + 1) * HEAD_DIM] = (
                o * inv * nw * (z * jax.nn.sigmoid(z))).astype(o_ref.dtype)
        return tuple(new_states)

    zero = jnp.zeros((HEAD_DIM, HEAD_DIM), F32)
    lax.fori_loop(0, seq // ch, chunk, (zero,) * rep)


def _gdn_core(qkv, z, beta, gc, norm_w, batch, seq):
    hk, hv = GDN_K_HEADS, GDN_V_HEADS
    rep = hv // hk
    wide = rep * HEAD_DIM
    v_off = 2 * hk * HEAD_DIM // wide
    return pl.pallas_call(
        functools.partial(_gdn_core_kernel, seq=seq),
        grid=(batch, hk),
        in_specs=[pl.BlockSpec((seq, HEAD_DIM), lambda b, h: (b, h)),
                  pl.BlockSpec((seq, HEAD_DIM), lambda b, h: (b, hk + h)),
                  pl.BlockSpec((seq, wide), lambda b, h: (b, v_off + h)),
                  pl.BlockSpec((seq, wide), lambda b, h: (b, h)),
                  pl.BlockSpec((seq, hv), lambda b, h: (b, 0)),
                  pl.BlockSpec((seq, hv), lambda b, h: (b, 0)),
                  pl.BlockSpec((1, HEAD_DIM), lambda b, h: (0, 0))],
        out_specs=pl.BlockSpec((seq, wide), lambda b, h: (b, h)),
        out_shape=jax.ShapeDtypeStruct((batch * seq, hv * HEAD_DIM), BF16),
        compiler_params=_params(2),
    )(qkv, qkv, qkv, z, beta, gc, norm_w.reshape(1, HEAD_DIM))


def _gated_deltanet(xb, w_in, conv_w, a_log, dt_bias, norm_w, w_o, batch, seq):
    key_dim = GDN_K_HEADS * HEAD_DIM
    val_dim = GDN_V_HEADS * HEAD_DIM
    conv_ch = 2 * key_dim + val_dim
    w16 = w_in[:, :conv_ch + val_dim].astype(BF16)
    w_ba = w_in[:, conv_ch + val_dim:].astype(BF16)
    pre = _dense_matmul(xb, w16, col0=0, n_cols=conv_ch)
    z = _dense_matmul(xb, w16, col0=conv_ch, n_cols=val_dim)
    ba = _dense_matmul(xb, w_ba)
    qkv = _gdn_conv(pre, conv_w, batch, seq)
    beta, gc = _gdn_gates(ba, a_log, dt_bias)
    og = _gdn_core(qkv, z, beta, gc, norm_w, batch, seq)
    return _dense_matmul(og, w_o.astype(BF16))


def kernel(x, sb_w_qkv, sb_w_o, gdn_w_in, gdn_conv_w, gdn_a_log, gdn_dt_bias, gdn_norm_w, gdn_w_o, ln_mix_g, ln_mix_b, ln_ffn_g, ln_ffn_b, moe_router_w, moe_router_b, moe_w_gu, moe_b_gu, moe_w_down, moe_b_down):
    batch, seq, d = x.shape
    xf = x.reshape(batch * seq, d)
    xb = xf.astype(BF16)
    for i in range(DEPTH):
        j = i // 2
        if i % 2 == 0:
            qkv = _dense_matmul(xb, sb_w_qkv[j].astype(BF16), out_dtype=BF16)
            attn = _sb_attention(qkv, batch, seq)
            mix = _dense_matmul(attn, sb_w_o[j].astype(BF16))
        else:
            mix = _gated_deltanet(xb, gdn_w_in[j], gdn_conv_w[j], gdn_a_log[j], gdn_dt_bias[j],
                                  gdn_norm_w[j], gdn_w_o[j], batch, seq)
        x1, e_idx, gate, rank, counts = _ln_router(xf, mix, ln_mix_g[i], ln_mix_b[i],
                                                   moe_router_w[i], moe_router_b[i])
        last = i == DEPTH - 1
        outs = _moe_layer(x1, e_idx, gate, rank, counts, moe_w_gu, moe_b_gu[i], moe_w_down,
                          moe_b_down[i], i, ln_ffn_g[i], ln_ffn_b[i], with_bf16=not last)
        xf = outs[0]
        if not last:
            xb = outs[1]
    return xf.reshape(batch, seq, d)
```

```python
import functools

import jax
import jax.numpy as jnp
from jax import lax
from jax.experimental import pallas as pl
from jax.experimental.pallas import tpu as pltpu

F32, BF16, I32 = jnp.float32, jnp.bfloat16, jnp.int32

DEPTH = 2
N_EXPERTS = 32
TOP_K = 4
HEAD_DIM = 128
SB_HEADS = 16
GDN_K_HEADS = 16
GDN_V_HEADS = 32
GDN_CONV = 4
GDN_CHUNK = 64
SWIGLU_LIMIT = 7.0
SWIGLU_ALPHA = 1.702
DEEPNORM_ALPHA = (2 * DEPTH) ** 0.25
LN_EPS = 1e-5
RMS_EPS = 1e-6
L2_EPS = 1e-6

VMEM_LIMIT_BYTES = 56 * 1024 * 1024
GDN_GROUP = 4
CONV_HALO = 16
MOE_ROWS = 256
ATTN_BLOCK = 256


def _params(n_axes):
    return pltpu.CompilerParams(dimension_semantics=("arbitrary",) * n_axes,
                                vmem_limit_bytes=VMEM_LIMIT_BYTES)


def _iota(shape, dim):
    return lax.broadcasted_iota(I32, shape, dim)


def _split_bf16(x):
    hi = x.astype(BF16)
    lo = (x - hi.astype(F32)).astype(BF16)
    return hi, lo


def _dot(a, b):
    return jnp.dot(a, b, preferred_element_type=F32)


def _dot_nt(a, b):
    return lax.dot_general(a, b, (((1,), (1,)), ((), ())), preferred_element_type=F32)


def _mm_kernel(a_ref, w_ref, o_ref):
    o_ref[...] = _dot(a_ref[...], w_ref[...]).astype(o_ref.dtype)


def _dense_matmul(a, w, *, col0=0, n_cols=None, out_dtype=F32, tm=512, tn=1024):
    m_rows, k_dim = a.shape
    n_cols = w.shape[1] if n_cols is None else n_cols
    tn = min(tn, n_cols)
    tm = min(tm, m_rows)
    assert n_cols % tn == 0 and col0 % tn == 0 and m_rows % tm == 0
    off = col0 // tn
    return pl.pallas_call(
        _mm_kernel,
        grid=(n_cols // tn, m_rows // tm),
        in_specs=[pl.BlockSpec((tm, k_dim), lambda n, m: (m, 0)),
                  pl.BlockSpec((k_dim, tn), lambda n, m: (0, n + off))],
        out_specs=pl.BlockSpec((tm, tn), lambda n, m: (m, n)),
        out_shape=jax.ShapeDtypeStruct((m_rows, n_cols), out_dtype),
        compiler_params=_params(2),
    )(a, w)


def _sb_attn_kernel(q_ref, k_ref, v_ref, o_ref, *, blk, scale):
    i = pl.program_id(2)
    q = q_ref[...]
    rows = _iota((blk, blk), 0)
    cols = _iota((blk, blk), 1)
    later = (rows > cols).astype(BF16)
    causal = cols < rows

    def visit(blocks, carry, acc, diagonal):
        n = len(blocks)
        ks = [k_ref[pl.ds(pl.multiple_of(j * blk, blk), blk), :] for j in blocks]
        vs = [v_ref[pl.ds(pl.multiple_of(j * blk, blk), blk), :] for j in blocks]
        zs = [_dot_nt(q, kb) * scale for kb in ks]
        log_beta, log_1m, parts = [], [], []
        for idx, z in enumerate(zs):
            lb = jnp.minimum(z, 0.0) - jnp.log(1.0 + jnp.exp(-jnp.abs(z)))
            lm = lb - z
            if diagonal and idx == 0:
                lm = jnp.where(causal, lm, 0.0)
            log_beta.append(lb)
            log_1m.append(lm)
            parts.extend(_split_bf16(lm))
        sums = _dot(jnp.concatenate(parts, axis=0), later)
        for idx in range(n):
            after = sums[2 * idx * blk:(2 * idx + 1) * blk] + sums[(2 * idx + 1) * blk:(2 * idx + 2) * blk]
            w = jnp.exp(log_beta[idx] + after + carry)
            if diagonal and idx == 0:
                w = jnp.where(causal, w, 0.0)
            acc = acc + _dot(w.astype(BF16), vs[idx])
            carry = carry + jnp.sum(log_1m[idx], axis=1, keepdims=True)
        return carry, acc

    start = (jnp.zeros((blk, 1), F32), jnp.zeros((blk, HEAD_DIM), F32))
    odd = i % 2
    carry, acc = lax.cond(odd == 1,
                          lambda: visit([i, i - 1], *start, True),
                          lambda: visit([i], *start, True))

    def body(t, state):
        j = i - odd - 1 - 2 * t
        return visit([j, j - 1], *state, False)

    carry, acc = lax.fori_loop(0, (i - odd) // 2, body, (carry, acc))
    o_ref[...] = acc.astype(o_ref.dtype)


def _sb_attention(qkv, batch, seq):
    blk = min(ATTN_BLOCK, seq)
    nq = seq // blk
    h = SB_HEADS
    kernel = functools.partial(_sb_attn_kernel, blk=blk, scale=HEAD_DIM ** -0.5)
    return pl.pallas_call(
        kernel,
        grid=(batch, h, nq),
        in_specs=[pl.BlockSpec((blk, HEAD_DIM), lambda b, hh, i: (b * nq + i, hh)),
                  pl.BlockSpec((seq, HEAD_DIM), lambda b, hh, i: (b, h + hh)),
                  pl.BlockSpec((seq, HEAD_DIM), lambda b, hh, i: (b, 2 * h + hh))],
        out_specs=pl.BlockSpec((blk, HEAD_DIM), lambda b, hh, i: (b * nq + i, hh)),
        out_shape=jax.ShapeDtypeStruct((batch * seq, h * HEAD_DIM), BF16),
        compiler_params=_params(3),
    )(qkv, qkv, qkv)


def _deepnorm_ln(x, upd, g, b):
    h = DEEPNORM_ALPHA * x + upd
    mu = jnp.mean(h, axis=-1, keepdims=True)
    hc = h - mu
    var = jnp.mean(hc * hc, axis=-1, keepdims=True)
    return hc * lax.rsqrt(var + LN_EPS) * g + b


def _ln_router_kernel(x_ref, u_ref, g_ref, b_ref, rw_ref, rb_ref,
                      xo_ref, e_ref, gate_ref, rank_ref, cnt_ref, base_ref, *, tm):
    i = pl.program_id(0)

    @pl.when(i == 0)
    def _():
        base_ref[...] = jnp.zeros_like(base_ref)

    y = _deepnorm_ln(x_ref[...], u_ref[...], g_ref[...], b_ref[...])
    xo_ref[...] = y

    yh, yl = _split_bf16(y)
    wh, wl = _split_bf16(rw_ref[...])
    logits = _dot(yh, wh) + _dot(yl, wh) + _dot(yh, wl) + rb_ref[...]

    lane = _iota((tm, N_EXPERTS), 1)
    slot = _iota((tm, TOP_K), 1)
    work = logits
    picks, tops = [], []
    chosen = jnp.zeros((tm, N_EXPERTS), F32)
    for _ in range(TOP_K):
        top = jnp.max(work, axis=1, keepdims=True)
        idx = jnp.min(jnp.where(work == top, lane, N_EXPERTS), axis=1, keepdims=True)
        hit = lane == idx
        chosen = jnp.where(hit, 1.0, chosen)
        work = jnp.where(hit, -jnp.inf, work)
        picks.append(idx)
        tops.append(top)

    exps = [jnp.exp(t - tops[0]) for t in tops]
    denom = exps[0] + exps[1] + exps[2] + exps[3]

    before = (_iota((tm, tm), 0) > _iota((tm, tm), 1)).astype(BF16)
    ahead = base_ref[...] + _dot(before, chosen.astype(BF16))

    e_out = jnp.zeros((tm, TOP_K), I32)
    g_out = jnp.zeros((tm, TOP_K), F32)
    r_out = jnp.zeros((tm, TOP_K), F32)
    for k in range(TOP_K):
        rank_k = jnp.sum(jnp.where(lane == picks[k], ahead, 0.0), axis=1, keepdims=True)
        e_out = jnp.where(slot == k, picks[k], e_out)
        g_out = jnp.where(slot == k, exps[k] / denom, g_out)
        r_out = jnp.where(slot == k, rank_k, r_out)
    e_ref[...] = e_out
    gate_ref[...] = g_out
    rank_ref[...] = r_out.astype(I32)

    base_ref[...] = base_ref[...] + jnp.sum(chosen, axis=0, keepdims=True)
    cnt_ref[...] = base_ref[...]


def _ln_router(x, upd, g, b, router_w, router_b, tm=256):
    m_rows, d = x.shape
    tm = min(tm, m_rows)
    row = lambda i: (i, 0)
    fixed = lambda i: (0, 0)
    return pl.pallas_call(
        functools.partial(_ln_router_kernel, tm=tm),
        grid=(m_rows // tm,),
        in_specs=[pl.BlockSpec((tm, d), row), pl.BlockSpec((tm, d), row),
                  pl.BlockSpec((1, d), fixed), pl.BlockSpec((1, d), fixed),
                  pl.BlockSpec((d, N_EXPERTS), fixed), pl.BlockSpec((1, N_EXPERTS), fixed)],
        out_specs=[pl.BlockSpec((tm, d), row), pl.BlockSpec((tm, TOP_K), row),
                   pl.BlockSpec((tm, TOP_K), row), pl.BlockSpec((tm, TOP_K), row),
                   pl.BlockSpec((1, N_EXPERTS), fixed)],
        out_shape=[jax.ShapeDtypeStruct((m_rows, d), F32),
                   jax.ShapeDtypeStruct((m_rows, TOP_K), I32),
                   jax.ShapeDtypeStruct((m_rows, TOP_K), F32),
                   jax.ShapeDtypeStruct((m_rows, TOP_K), I32),
                   jax.ShapeDtypeStruct((1, N_EXPERTS), F32)],
        scratch_shapes=[pltpu.VMEM((1, N_EXPERTS), F32)],
        compiler_params=_params(1),
    )(x, upd, g.reshape(1, d), b.reshape(1, d), router_w, router_b.reshape(1, N_EXPERTS))


def _row_copy(src_hbm, row, dst, sem):
    return pltpu.make_async_copy(src_hbm.at[pl.ds(row, 1), :], dst, sem)


def _dispatch_kernel(tok_ref, meta_ref, x_hbm, o_ref, buf, sem, *, rb):
    i = pl.program_id(0)
    n_active = meta_ref[0]

    def start(blk, slot):
        def body(r, c):
            _row_copy(x_hbm, tok_ref[blk * rb + r], buf.at[slot, pl.ds(r, 1), :], sem.at[slot]).start()
            return c
        lax.fori_loop(0, rb, body, 0, unroll=16)

    @pl.when(i == 0)
    def _():
        start(0, 0)

    @pl.when(i + 1 < n_active)
    def _():
        start(i + 1, (i + 1) % 2)

    @pl.when(i < n_active)
    def _():
        slot = i % 2
        pltpu.make_async_copy(x_hbm.at[pl.ds(0, rb), :], buf.at[slot], sem.at[slot]).wait()
        o_ref[...] = buf[slot].astype(BF16)

    @pl.when(i >= n_active)
    def _():
        o_ref[...] = jnp.zeros_like(o_ref)


def _dispatch(x, row_tok, meta, rb):
    m_rows, d = x.shape
    n_rows = row_tok.shape[0]
    return pl.pallas_call(
        functools.partial(_dispatch_kernel, rb=rb),
        grid_spec=pltpu.PrefetchScalarGridSpec(
            num_scalar_prefetch=2,
            grid=(n_rows // rb,),
            in_specs=[pl.BlockSpec(memory_space=pl.ANY)],
            out_specs=pl.BlockSpec((rb, d), lambda i, tok, meta: (i, 0)),
            scratch_shapes=[pltpu.VMEM((2, rb, d), F32), pltpu.SemaphoreType.DMA((2,))]),
        out_shape=jax.ShapeDtypeStruct((n_rows, d), BF16),
        compiler_params=_params(1),
    )(row_tok, meta, x)


def _expert_weights(be_ref, nxt_ref, gidx_ref, meta_ref, w_hbm, wbuf, sem, *, layer, w_cols):
    n = pl.program_id(0)
    m = pl.program_id(1)
    n_tiles = pl.num_programs(0)
    n_active, n_groups = meta_ref[0], meta_ref[1]
    active = m < n_active
    fresh = jnp.logical_and(active, jnp.logical_or(m == 0, be_ref[m] != be_ref[jnp.maximum(m - 1, 0)]))
    half = (n * n_groups + gidx_ref[m]) % 2

    def tile(e, col, h):
        cols = pl.ds(pl.multiple_of(col * w_cols, w_cols), w_cols)
        return pltpu.make_async_copy(w_hbm.at[layer, e, :, cols], wbuf.at[h], sem.at[h])

    @pl.when(jnp.logical_and(fresh, jnp.logical_and(n == 0, m == 0)))
    def _():
        tile(be_ref[0], 0, 0).start()

    @pl.when(fresh)
    def _():
        nxt_col = jnp.where(gidx_ref[m] == n_groups - 1, n + 1, n)

        @pl.when(nxt_col < n_tiles)
        def _():
            tile(nxt_ref[m], nxt_col, 1 - half).start()

        tile(be_ref[m], n, half).wait()

    return active, fresh, half


def _gate_up_kernel(be_ref, nxt_ref, gidx_ref, meta_ref, a_ref, w_hbm, bg_ref, bl_ref, o_ref,
                    wbuf, sem, wg_s, wl_s, *, tf, layer):
    active, fresh, half = _expert_weights(be_ref, nxt_ref, gidx_ref, meta_ref, w_hbm, wbuf, sem,
                                          layer=layer, w_cols=2 * tf)
    grp = 2 * HEAD_DIM

    @pl.when(fresh)
    def _():
        r = _iota((grp, grp), 0)
        c = _iota((grp, grp), 1)
        source = jnp.where(c < HEAD_DIM, 2 * c, 2 * (c - HEAD_DIM) + 1)
        pick = (r == source).astype(BF16)
        for g in range(2 * tf // grp):
            wb = wbuf[half, :, g * grp:(g + 1) * grp].astype(BF16)
            sel = _dot(wb, pick)
            wg_s[:, g * HEAD_DIM:(g + 1) * HEAD_DIM] = sel[:, :HEAD_DIM].astype(BF16)
            wl_s[:, g * HEAD_DIM:(g + 1) * HEAD_DIM] = sel[:, HEAD_DIM:].astype(BF16)

    @pl.when(active)
    def _():
        a = a_ref[...]
        glu = jnp.minimum(_dot(a, wg_s[...]) + bg_ref[0], SWIGLU_LIMIT)
        lin = jnp.clip(_dot(a, wl_s[...]) + bl_ref[0], -SWIGLU_LIMIT, SWIGLU_LIMIT)
        act = glu * jax.nn.sigmoid(SWIGLU_ALPHA * glu) * (lin + 1.0)
        o_ref[...] = act.astype(o_ref.dtype)

    @pl.when(jnp.logical_not(active))
    def _():
        o_ref[...] = jnp.zeros_like(o_ref)


def _down_kernel(be_ref, nxt_ref, gidx_ref, meta_ref, a_ref, w_hbm, b_ref, o_ref, wbuf, sem, w_s,
                 *, tn, layer):
    active, fresh, half = _expert_weights(be_ref, nxt_ref, gidx_ref, meta_ref, w_hbm, wbuf, sem,
                                          layer=layer, w_cols=tn)

    @pl.when(fresh)
    def _():
        w_s[...] = wbuf[half].astype(BF16)

    @pl.when(active)
    def _():
        o_ref[...] = (_dot(a_ref[...], w_s[...]) + b_ref[0]).astype(o_ref.dtype)

    @pl.when(jnp.logical_not(active))
    def _():
        o_ref[...] = jnp.zeros_like(o_ref)


def _grouped_specs(rb, k_dim, b_cols, n_bias):
    def m_eff(m, meta):
        return jnp.minimum(m, meta[0] - 1)
    a_spec = pl.BlockSpec((rb, k_dim), lambda n, m, be, nxt, gidx, meta: (m_eff(m, meta), 0))
    w_spec = pl.BlockSpec(memory_space=pl.ANY)
    b_spec = pl.BlockSpec((1, 1, b_cols), lambda n, m, be, nxt, gidx, meta: (be[m_eff(m, meta)], 0, n))
    o_spec = pl.BlockSpec((rb, b_cols), lambda n, m, be, nxt, gidx, meta: (m, n))
    return [a_spec, w_spec] + [b_spec] * n_bias, o_spec


def _moe_gate_up(xs, w_gu, layer, b_glu, b_lin, tables, rb, tf=512):
    n_rows, d = xs.shape
    ff = w_gu.shape[3] // 2
    in_specs, o_spec = _grouped_specs(rb, d, tf, 2)
    return pl.pallas_call(
        functools.partial(_gate_up_kernel, tf=tf, layer=layer),
        grid_spec=pltpu.PrefetchScalarGridSpec(
            num_scalar_prefetch=4, grid=(ff // tf, n_rows // rb),
            in_specs=in_specs, out_specs=o_spec,
            scratch_shapes=[pltpu.VMEM((2, d, 2 * tf), F32), pltpu.SemaphoreType.DMA((2,)),
                            pltpu.VMEM((d, tf), BF16), pltpu.VMEM((d, tf), BF16)]),
        out_shape=jax.ShapeDtypeStruct((n_rows, ff), BF16),
        compiler_params=_params(2),
    )(*tables, xs, w_gu, b_glu, b_lin)


def _moe_down(act, w_down, layer, b_down, tables, rb, tn=1024):
    n_rows, ff = act.shape
    d = w_down.shape[3]
    in_specs, o_spec = _grouped_specs(rb, ff, tn, 1)
    return pl.pallas_call(
        functools.partial(_down_kernel, tn=tn, layer=layer),
        grid_spec=pltpu.PrefetchScalarGridSpec(
            num_scalar_prefetch=4, grid=(d // tn, n_rows // rb),
            in_specs=in_specs, out_specs=o_spec,
            scratch_shapes=[pltpu.VMEM((2, ff, tn), F32), pltpu.SemaphoreType.DMA((2,)),
                            pltpu.VMEM((ff, tn), BF16)]),
        out_shape=jax.ShapeDtypeStruct((n_rows, d), F32),
        compiler_params=_params(2),
    )(*tables, act, w_down, b_down)


def _combine_kernel(dest_ref, y_hbm, x_ref, gate_ref, g_ref, b_ref, *rest, tm, with_bf16):
    if with_bf16:
        xo_ref, xb_ref, buf, sem = rest
    else:
        xo_ref, buf, sem = rest
    i = pl.program_id(0)
    nb = pl.num_programs(0)
    n_slots = tm * TOP_K

    def start(blk, slot):
        def body(tok, c):
            for k in range(TOP_K):
                row = dest_ref[blk * n_slots + tok * TOP_K + k]
                _row_copy(y_hbm, row, buf.at[slot, k, pl.ds(tok, 1), :], sem.at[slot]).start()
            return c
        lax.fori_loop(0, tm, body, 0, unroll=4)

    @pl.when(i == 0)
    def _():
        start(0, 0)

    @pl.when(i + 1 < nb)
    def _():
        start(i + 1, (i + 1) % 2)

    for k in range(TOP_K):
        pltpu.make_async_copy(y_hbm.at[pl.ds(0, tm), :], buf.at[i % 2, k], sem.at[i % 2]).wait()
    gate = gate_ref[...]
    cur = buf.at[i % 2]
    ffn = gate[:, 0:1] * cur[0]
    for k in range(1, TOP_K):
        ffn = ffn + gate[:, k:k + 1] * cur[k]
    y = _deepnorm_ln(x_ref[...], ffn, g_ref[...], b_ref[...])
    xo_ref[...] = y
    if with_bf16:
        xb_ref[...] = y.astype(BF16)


def _combine(yb, dest, x, gate, g, b, with_bf16, tm=128):
    m_rows, d = x.shape
    tm = min(tm, m_rows)
    row = lambda i, dst: (i, 0)
    fixed = lambda i, dst: (0, 0)
    out_specs = [pl.BlockSpec((tm, d), row)]
    out_shape = [jax.ShapeDtypeStruct((m_rows, d), F32)]
    if with_bf16:
        out_specs.append(pl.BlockSpec((tm, d), row))
        out_shape.append(jax.ShapeDtypeStruct((m_rows, d), BF16))
    return pl.pallas_call(
        functools.partial(_combine_kernel, tm=tm, with_bf16=with_bf16),
        grid_spec=pltpu.PrefetchScalarGridSpec(
            num_scalar_prefetch=1, grid=(m_rows // tm,),
            in_specs=[pl.BlockSpec(memory_space=pl.ANY), pl.BlockSpec((tm, d), row),
                      pl.BlockSpec((tm, TOP_K), row), pl.BlockSpec((1, d), fixed), pl.BlockSpec((1, d), fixed)],
            out_specs=out_specs,
            scratch_shapes=[pltpu.VMEM((2, TOP_K, tm, d), F32), pltpu.SemaphoreType.DMA((2,))]),
        out_shape=out_shape,
        compiler_params=_params(1),
    )(dest, yb, x, gate, g.reshape(1, d), b.reshape(1, d))


def _moe_layer(x1, e_idx, gate, rank, counts, w_gu, b_gu, w_down, b_down, layer, ln_g, ln_b, with_bf16):
    m_rows, d = x1.shape
    rb = MOE_ROWS
    n_slots = m_rows * TOP_K
    n_blocks = -(-n_slots // rb) + N_EXPERTS
    n_rows = n_blocks * rb
    cnt = counts.reshape(N_EXPERTS).astype(I32)
    blocks_e = (cnt + rb - 1) // rb
    blk_end = jnp.cumsum(blocks_e)
    row_start = (blk_end - blocks_e) * rb
    dest = (row_start[e_idx] + rank).reshape(n_slots)
    blk = jnp.arange(n_blocks, dtype=I32)
    block_e = jnp.minimum(jnp.sum((blk_end[None, :] <= blk[:, None]).astype(I32), axis=1), N_EXPERTS - 1)
    ids = jnp.arange(N_EXPERTS, dtype=I32)
    owns = blocks_e > 0
    group_of_e = jnp.cumsum(owns.astype(I32)) - 1
    later = jnp.where(jnp.logical_and(owns[None, :], ids[None, :] > ids[:, None]), ids[None, :], N_EXPERTS)
    first = jnp.min(jnp.where(owns, ids, N_EXPERTS))
    next_e = jnp.min(later, axis=1)
    next_e = jnp.where(next_e == N_EXPERTS, first, next_e)
    meta = jnp.stack([blk_end[-1], group_of_e[-1] + 1]).astype(I32)
    tables = (block_e, next_e[block_e].astype(I32), group_of_e[block_e].astype(I32), meta)
    slot_tok = jnp.arange(n_slots, dtype=I32) // TOP_K
    row_tok = jnp.zeros((n_rows,), I32).at[dest].set(slot_tok)

    ff = w_down.shape[2]
    b_pairs = b_gu.reshape(N_EXPERTS, ff, 2)
    b_glu = b_pairs[:, :, 0].reshape(N_EXPERTS, 1, ff)
    b_lin = b_pairs[:, :, 1].reshape(N_EXPERTS, 1, ff)

    xs = _dispatch(x1, row_tok, meta, rb)
    act = _moe_gate_up(xs, w_gu, layer, b_glu, b_lin, tables, rb)
    yb = _moe_down(act, w_down, layer, b_down.reshape(N_EXPERTS, 1, d), tables, rb)
    return _combine(yb, dest, x1, gate, ln_g, ln_b, with_bf16)


def _conv_kernel(x_ref, halo_ref, w_ref, o_ref, ext, *, tt, nt, n_key_blocks, heads_per_block):
    i = pl.program_id(0)
    c = pl.program_id(1)
    first = (i % nt) == 0
    x = x_ref[...].astype(F32)
    ext[0:CONV_HALO, :] = jnp.where(first, 0.0, halo_ref[...].astype(F32))
    ext[CONV_HALO:, :] = x
    w = w_ref[...]
    acc = x * w[GDN_CONV - 1:GDN_CONV, :]
    for j in range(GDN_CONV - 1):
        acc = acc + ext[pl.ds(CONV_HALO - (GDN_CONV - 1) + j, tt), :] * w[j:j + 1, :]
    y = acc * jax.nn.sigmoid(acc)

    def normed(scale):
        for hh in range(heads_per_block):
            seg = y[:, hh * HEAD_DIM:(hh + 1) * HEAD_DIM]
            inv = lax.rsqrt(jnp.sum(seg * seg, axis=1, keepdims=True) + L2_EPS)
            o_ref[:, hh * HEAD_DIM:(hh + 1) * HEAD_DIM] = (seg * (inv * scale)).astype(o_ref.dtype)

    @pl.when(c < n_key_blocks)
    def _():
        normed(HEAD_DIM ** -0.5)

    @pl.when(jnp.logical_and(c >= n_key_blocks, c < 2 * n_key_blocks))
    def _():
        normed(1.0)

    @pl.when(c >= 2 * n_key_blocks)
    def _():
        o_ref[...] = y.astype(o_ref.dtype)


def _gdn_conv(pre, conv_w, batch, seq, tt=256, cb=1024):
    m_rows, ch = pre.shape
    tt = min(tt, seq)
    nt = seq // tt
    key_dim = GDN_K_HEADS * HEAD_DIM
    kernel = functools.partial(_conv_kernel, tt=tt, nt=nt, n_key_blocks=key_dim // cb,
                               heads_per_block=cb // HEAD_DIM)
    return pl.pallas_call(
        kernel,
        grid=(m_rows // tt, ch // cb),
        in_specs=[pl.BlockSpec((tt, cb), lambda i, c: (i, c)),
                  pl.BlockSpec((CONV_HALO, cb), lambda i, c: (jnp.maximum(i * (tt // CONV_HALO) - 1, 0), c)),
                  pl.BlockSpec((GDN_CONV, cb), lambda i, c: (0, c))],
        out_specs=pl.BlockSpec((tt, cb), lambda i, c: (i, c)),
        out_shape=jax.ShapeDtypeStruct((m_rows, ch), BF16),
        scratch_shapes=[pltpu.VMEM((tt + CONV_HALO, cb), F32)],
        compiler_params=_params(2),
    )(pre, pre, conv_w)


def _gates_kernel(ba_ref, alog_ref, dtb_ref, beta_ref, gc_ref, *, tm):
    ba = ba_ref[...]
    beta_ref[...] = jax.nn.sigmoid(ba[:, :GDN_V_HEADS])
    pre = ba[:, GDN_V_HEADS:] + dtb_ref[...]
    softplus = jnp.maximum(pre, 0.0) + jnp.log(1.0 + jnp.exp(-jnp.abs(pre)))
    g = -jnp.exp(alog_ref[...]) * softplus
    r = _iota((tm, tm), 0)
    c = _iota((tm, tm), 1)
    upto = jnp.logical_and(r >= c, r // GDN_CHUNK == c // GDN_CHUNK).astype(BF16)
    g1 = g.astype(BF16)
    rem = g - g1.astype(F32)
    g2 = rem.astype(BF16)
    g3 = (rem - g2.astype(F32)).astype(BF16)
    gc_ref[...] = _dot(upto, g1) + _dot(upto, g2) + _dot(upto, g3)


def _gdn_gates(ba, a_log, dt_bias, tm=256):
    m_rows = ba.shape[0]
    tm = min(tm, m_rows)
    hv = GDN_V_HEADS
    return pl.pallas_call(
        functools.partial(_gates_kernel, tm=tm),
        grid=(m_rows // tm,),
        in_specs=[pl.BlockSpec((tm, 2 * hv), lambda i: (i, 0)),
                  pl.BlockSpec((1, hv), lambda i: (0, 0)), pl.BlockSpec((1, hv), lambda i: (0, 0))],
        out_specs=[pl.BlockSpec((tm, hv), lambda i: (i, 0)), pl.BlockSpec((tm, hv), lambda i: (i, 0))],
        out_shape=[jax.ShapeDtypeStruct((m_rows, hv), F32), jax.ShapeDtypeStruct((m_rows, hv), F32)],
        compiler_params=_params(1),
    )(ba, a_log.reshape(1, hv), dt_bias.reshape(1, hv))


def _gdn_core_kernel(q_ref, k_ref, v_ref, z_ref, beta_ref, gc_ref, nw_ref, o_ref,
                     inject_s, mix_s, local_s, read_s, keep_s, state_s, *, seq):
    hk = pl.program_id(1)
    ch = GDN_CHUNK
    n_chunks = seq // ch
    rep = GDN_V_HEADS // GDN_K_HEADS
    ri = _iota((ch, ch), 0)
    ci = _iota((ch, ch), 1)
    incl, strict, eye = ri >= ci, ri > ci, ri == ci
    head_lane = _iota((ch, GDN_V_HEADS), 1)
    nw = nw_ref[...]

    group = min(GDN_GROUP, n_chunks)

    def local_terms(t, carry):
        probs = []
        prods = []
        for c in range(group):
            n = t * group + c
            rows = pl.ds(pl.multiple_of(n * ch, ch), ch)
            q = q_ref[rows, :].astype(F32)
            k = k_ref[rows, :].astype(F32)
            beta_all = beta_ref[rows, :]
            gc_all = gc_ref[rows, :]
            kbetas = []
            for j in range(rep):
                pick = head_lane == (rep * hk + j)
                bj = jnp.sum(jnp.where(pick, beta_all, 0.0), axis=1, keepdims=True)
                gj = jnp.sum(jnp.where(pick, gc_all, 0.0), axis=1, keepdims=True)
                grow = jnp.sum(jnp.where(eye, gj, 0.0), axis=0, keepdims=True)
                dj = jnp.where(incl, jnp.exp(jnp.where(incl, gj - grow, 0.0)), 0.0)
                kbetas.append(k * bj)
                probs.append(dict(c=c, j=j, n=n, rows=rows, q=q, k=k, bj=bj, gj=gj, dj=dj, kbeta=kbetas[-1]))
            stacked = jnp.concatenate([kb.astype(BF16) for kb in kbetas] + [q.astype(BF16)], axis=0)
            prods.append(_dot_nt(stacked, k.astype(BF16)))
        for p in probs:
            pr = prods[p["c"]]
            j = p["j"]
            p["pm"] = -jnp.where(strict, pr[j * ch:(j + 1) * ch, :] * p["dj"], 0.0)
            p["qk"] = jnp.where(incl, pr[rep * ch:, :] * p["dj"], 0.0)
        for p in probs:
            n16 = p["pm"].astype(BF16)
            p["power"] = _dot(n16, n16)
        for step in range(5):
            for p in probs:
                p16 = p["power"].astype(BF16)
                if step < 4:
                    both = _dot(jnp.concatenate([p16, p["pm"].astype(BF16)], axis=0), p16)
                    p["pm"] = p["pm"] + p["power"] + both[ch:, :]
                    p["power"] = both[:ch, :]
                else:
                    p["pm"] = p["pm"] + p["power"] + _dot(p["pm"].astype(BF16), p16)
        for p in probs:
            j, gj = p["j"], p["gj"]
            p["eg"] = jnp.exp(gj)
            v = v_ref[p["rows"], j * HEAD_DIM:(j + 1) * HEAD_DIM].astype(F32)
            rhs = jnp.concatenate([v * p["bj"], p["kbeta"] * p["eg"]], axis=1)
            p["uw"] = rhs + _dot(p["pm"].astype(BF16), rhs.astype(BF16))
        for p in probs:
            j, n, gj = p["j"], p["n"], p["gj"]
            g_last = gj[ch - 1:ch, :]
            k_dec = p["k"] * jnp.exp(g_last - gj)
            lhs = jnp.concatenate([k_dec.T.astype(BF16), p["qk"].astype(BF16)], axis=0)
            big = _dot(lhs, p["uw"].astype(BF16))
            inject_s[j, n] = big[:HEAD_DIM, :HEAD_DIM]
            mix_s[j, n] = big[:HEAD_DIM, HEAD_DIM:].astype(BF16)
            local_s[j, p["rows"], :] = big[HEAD_DIM:, :HEAD_DIM]
            read_s[j, p["rows"], :] = (p["q"] * p["eg"] - big[HEAD_DIM:, HEAD_DIM:]).astype(BF16)
            keep_s[j, n] = jnp.broadcast_to(jnp.exp(g_last), (1, HEAD_DIM))
        return carry

    lax.fori_loop(0, n_chunks // group, local_terms, 0)

    def advance(n, states):
        nxt = []
        for j in range(rep):
            s16 = states[j].astype(BF16)
            state_s[j, n] = s16
            nxt.append(keep_s[j, n] * states[j] - _dot(mix_s[j, n], s16) + inject_s[j, n])
        return tuple(nxt)

    zero = jnp.zeros((HEAD_DIM, HEAD_DIM), F32)
    lax.fori_loop(0, n_chunks, advance, (zero,) * rep)

    def outputs(n, carry):
        rows = pl.ds(pl.multiple_of(n * ch, ch), ch)
        for j in range(rep):
            o = _dot(read_s[j, rows, :], state_s[j, n]) + local_s[j, rows, :]
            z = z_ref[rows, j * HEAD_DIM:(j + 1) * HEAD_DIM].astype(F32)
            inv = lax.rsqrt(jnp.mean(o * o, axis=1, keepdims=True) + RMS_EPS)
            o_ref[rows, j * HEAD_DIM:(j + 1) * HEAD_DIM] = (
                o * inv * nw * (z * jax.nn.sigmoid(z))).astype(o_ref.dtype)
        return carry

    lax.fori_loop(0, n_chunks, outputs, 0, unroll=4)


def _gdn_core(qkv, z, beta, gc, norm_w, batch, seq):
    hk, hv = GDN_K_HEADS, GDN_V_HEADS
    rep = hv // hk
    wide = rep * HEAD_DIM
    v_off = 2 * hk * HEAD_DIM // wide
    n_chunks = seq // GDN_CHUNK
    return pl.pallas_call(
        functools.partial(_gdn_core_kernel, seq=seq),
        grid=(batch, hk),
        in_specs=[pl.BlockSpec((seq, HEAD_DIM), lambda b, h: (b, h)),
                  pl.BlockSpec((seq, HEAD_DIM), lambda b, h: (b, hk + h)),
                  pl.BlockSpec((seq, wide), lambda b, h: (b, v_off + h)),
                  pl.BlockSpec((seq, wide), lambda b, h: (b, h)),
                  pl.BlockSpec((seq, hv), lambda b, h: (b, 0)),
                  pl.BlockSpec((seq, hv), lambda b, h: (b, 0)),
                  pl.BlockSpec((1, HEAD_DIM), lambda b, h: (0, 0))],
        out_specs=pl.BlockSpec((seq, wide), lambda b, h: (b, h)),
        out_shape=jax.ShapeDtypeStruct((batch * seq, hv * HEAD_DIM), BF16),
        scratch_shapes=[pltpu.VMEM((rep, n_chunks, HEAD_DIM, HEAD_DIM), F32),
                        pltpu.VMEM((rep, n_chunks, HEAD_DIM, HEAD_DIM), BF16),
                        pltpu.VMEM((rep, seq, HEAD_DIM), F32),
                        pltpu.VMEM((rep, seq, HEAD_DIM), BF16),
                        pltpu.VMEM((rep, n_chunks, 1, HEAD_DIM), F32),
                        pltpu.VMEM((rep, n_chunks, HEAD_DIM, HEAD_DIM), BF16)],
        compiler_params=_params(2),
    )(qkv, qkv, qkv, z, beta, gc, norm_w.reshape(1, HEAD_DIM))


def _gated_deltanet(xb, w_in, conv_w, a_log, dt_bias, norm_w, w_o, batch, seq):
    key_dim = GDN_K_HEADS * HEAD_DIM
    val_dim = GDN_V_HEADS * HEAD_DIM
    conv_ch = 2 * key_dim + val_dim
    w16 = w_in[:, :conv_ch + val_dim].astype(BF16)
    w_ba = w_in[:, conv_ch + val_dim:].astype(BF16)
    pre = _dense_matmul(xb, w16, col0=0, n_cols=conv_ch, out_dtype=BF16)
    z = _dense_matmul(xb, w16, col0=conv_ch, n_cols=val_dim, out_dtype=BF16)
    ba = _dense_matmul(xb, w_ba)
    qkv = _gdn_conv(pre, conv_w, batch, seq)
    beta, gc = _gdn_gates(ba, a_log, dt_bias)
    og = _gdn_core(qkv, z, beta, gc, norm_w, batch, seq)
    return _dense_matmul(og, w_o.astype(BF16))


def kernel(x, sb_w_qkv, sb_w_o, gdn_w_in, gdn_conv_w, gdn_a_log, gdn_dt_bias, gdn_norm_w, gdn_w_o, ln_mix_g, ln_mix_b, ln_ffn_g, ln_ffn_b, moe_router_w, moe_router_b, moe_w_gu, moe_b_gu, moe_w_down, moe_b_down):
    batch, seq, d = x.shape
    xf = x.reshape(batch * seq, d)
    xb = xf.astype(BF16)
    for i in range(DEPTH):
        j = i // 2
        if i % 2 == 0:
            qkv = _dense_matmul(xb, sb_w_qkv[j].astype(BF16), out_dtype=BF16)
            attn = _sb_attention(qkv, batch, seq)
            mix = _dense_matmul(attn, sb_w_o[j].astype(BF16))
        else:
            mix = _gated_deltanet(xb, gdn_w_in[j], gdn_conv_w[j], gdn_a_log[j], gdn_dt_bias[j],
                                  gdn_norm_w[j], gdn_w_o[j], batch, seq)
        x1, e_idx, gate, rank, counts = _ln_router(xf, mix, ln_mix_g[i], ln_mix_b[i],
                                                   moe_router_w[i], moe_router_b[i])
        last = i == DEPTH - 1
        outs = _moe_layer(x1, e_idx, gate, rank, counts, moe_w_gu, moe_b_gu[i], moe_w_down,
                          moe_b_down[i], i, ln_ffn_g[i], ln_ffn_b[i], with_bf16=not last)
        xf = outs[0]
        if not last:
            xb = outs[1]
    return xf.reshape(batch, seq, d)
```

```python
import functools

import jax
import jax.numpy as jnp
from jax import lax
from jax.experimental import pallas as pl
from jax.experimental.pallas import tpu as pltpu

F32, BF16, I32 = jnp.float32, jnp.bfloat16, jnp.int32

DEPTH = 2
N_EXPERTS = 32
TOP_K = 4
HEAD_DIM = 128
SB_HEADS = 16
GDN_K_HEADS = 16
GDN_V_HEADS = 32
GDN_CONV = 4
GDN_CHUNK = 64
SWIGLU_LIMIT = 7.0
SWIGLU_ALPHA = 1.702
DEEPNORM_ALPHA = (2 * DEPTH) ** 0.25
LN_EPS = 1e-5
RMS_EPS = 1e-6
L2_EPS = 1e-6

VMEM_LIMIT_BYTES = 56 * 1024 * 1024
GDN_GROUP = 8
CONV_HALO = 16
MOE_ROWS = 256
ATTN_BLOCK = 256
ATTN_SPAN = 4


def _params(n_axes):
    return pltpu.CompilerParams(dimension_semantics=("arbitrary",) * n_axes,
                                vmem_limit_bytes=VMEM_LIMIT_BYTES)


def _iota(shape, dim):
    return lax.broadcasted_iota(I32, shape, dim)


def _split_bf16(x):
    hi = x.astype(BF16)
    lo = (x - hi.astype(F32)).astype(BF16)
    return hi, lo


def _dot(a, b):
    return jnp.dot(a, b, preferred_element_type=F32)


def _dot_nt(a, b):
    return lax.dot_general(a, b, (((1,), (1,)), ((), ())), preferred_element_type=F32)


def _mm_kernel(a_ref, w_ref, o_ref, w_s):
    @pl.when(pl.program_id(1) == 0)
    def _():
        w_s[...] = w_ref[...].astype(BF16)

    o_ref[...] = _dot(a_ref[...], w_s[...]).astype(o_ref.dtype)


def _dense_matmul(a, w, *, lead=(), col0=0, n_cols=None, out_dtype=F32, tm=512, tn=1024):
    m_rows, k_dim = a.shape
    n_cols = w.shape[-1] if n_cols is None else n_cols
    tn = min(tn, n_cols)
    tm = min(tm, m_rows)
    assert n_cols % tn == 0 and col0 % tn == 0 and m_rows % tm == 0
    off = col0 // tn
    assert len(lead) == w.ndim - 2
    return pl.pallas_call(
        _mm_kernel,
        grid=(n_cols // tn, m_rows // tm),
        in_specs=[pl.BlockSpec((tm, k_dim), lambda n, m: (m, 0)),
                  pl.BlockSpec((None,) * len(lead) + (k_dim, tn), lambda n, m: lead + (0, n + off))],
        out_specs=pl.BlockSpec((tm, tn), lambda n, m: (m, n)),
        out_shape=jax.ShapeDtypeStruct((m_rows, n_cols), out_dtype),
        scratch_shapes=[pltpu.VMEM((k_dim, tn), BF16)],
        compiler_params=_params(2),
    )(a, w)


def _sb_attn_kernel(q_ref, k_ref, v_ref, o_ref, *, blk, span, scale):
    i = pl.program_id(2)
    q = q_ref[...]
    rows = _iota((blk, blk), 0)
    cols = _iota((blk, blk), 1)
    later = (rows > cols).astype(BF16)
    causal = cols < rows

    def visit(blocks, carry, acc, diagonal):
        n = len(blocks)
        ks = [k_ref[pl.ds(pl.multiple_of(j * blk, blk), blk), :] for j in blocks]
        vs = [v_ref[pl.ds(pl.multiple_of(j * blk, blk), blk), :] for j in blocks]
        zs = [_dot_nt(q, kb) * scale for kb in ks]
        log_beta, log_1m, parts = [], [], []
        for idx, z in enumerate(zs):
            lb = jnp.minimum(z, 0.0) - jnp.log(1.0 + jnp.exp(-jnp.abs(z)))
            lm = lb - z
            if diagonal and idx == 0:
                lm = jnp.where(causal, lm, 0.0)
            log_beta.append(lb)
            log_1m.append(lm)
            parts.extend(_split_bf16(lm))
        sums = _dot(jnp.concatenate(parts, axis=0), later)
        for idx in range(n):
            after = sums[2 * idx * blk:(2 * idx + 1) * blk] + sums[(2 * idx + 1) * blk:(2 * idx + 2) * blk]
            w = jnp.exp(log_beta[idx] + after + carry)
            if diagonal and idx == 0:
                w = jnp.where(causal, w, 0.0)
            acc = acc + _dot(w.astype(BF16), vs[idx])
            carry = carry + jnp.sum(log_1m[idx], axis=1, keepdims=True)
        return carry, acc

    start = (jnp.zeros((blk, 1), F32), jnp.zeros((blk, HEAD_DIM), F32))
    extra = i % span
    first = [functools.partial(visit, [i - s for s in range(n + 1)], *start, True) for n in range(span)]
    carry, acc = lax.switch(extra, first)

    def body(t, state):
        j = i - extra - 1 - span * t
        return visit([j - s for s in range(span)], *state, False)

    carry, acc = lax.fori_loop(0, (i - extra) // span, body, (carry, acc))
    o_ref[...] = acc.astype(o_ref.dtype)


def _sb_attention(qkv, batch, seq):
    blk = min(ATTN_BLOCK, seq)
    nq = seq // blk
    h = SB_HEADS
    kernel = functools.partial(_sb_attn_kernel, blk=blk, span=min(ATTN_SPAN, nq), scale=HEAD_DIM ** -0.5)
    return pl.pallas_call(
        kernel,
        grid=(batch, h, nq),
        in_specs=[pl.BlockSpec((blk, HEAD_DIM), lambda b, hh, i: (b * nq + i, hh)),
                  pl.BlockSpec((seq, HEAD_DIM), lambda b, hh, i: (b, h + hh)),
                  pl.BlockSpec((seq, HEAD_DIM), lambda b, hh, i: (b, 2 * h + hh))],
        out_specs=pl.BlockSpec((blk, HEAD_DIM), lambda b, hh, i: (b * nq + i, hh)),
        out_shape=jax.ShapeDtypeStruct((batch * seq, h * HEAD_DIM), BF16),
        compiler_params=_params(3),
    )(qkv, qkv, qkv)


def _deepnorm_ln(x, upd, g, b):
    h = DEEPNORM_ALPHA * x + upd
    mu = jnp.mean(h, axis=-1, keepdims=True)
    hc = h - mu
    var = jnp.mean(hc * hc, axis=-1, keepdims=True)
    return hc * lax.rsqrt(var + LN_EPS) * g + b


def _ln_router_kernel(x_ref, u_ref, g_ref, b_ref, rw_ref, rb_ref,
                      xo_ref, e_ref, gate_ref, rank_ref, cnt_ref, base_ref, *, tm):
    i = pl.program_id(0)

    @pl.when(i == 0)
    def _():
        base_ref[...] = jnp.zeros_like(base_ref)

    y = _deepnorm_ln(x_ref[...], u_ref[...], g_ref[...], b_ref[...])
    xo_ref[...] = y

    yh, yl = _split_bf16(y)
    wh, wl = _split_bf16(rw_ref[...])
    logits = _dot(yh, wh) + _dot(yl, wh) + _dot(yh, wl) + rb_ref[...]

    lane = _iota((tm, N_EXPERTS), 1)
    slot = _iota((tm, TOP_K), 1)
    work = logits
    picks, tops = [], []
    chosen = jnp.zeros((tm, N_EXPERTS), F32)
    for _ in range(TOP_K):
        top = jnp.max(work, axis=1, keepdims=True)
        idx = jnp.min(jnp.where(work == top, lane, N_EXPERTS), axis=1, keepdims=True)
        hit = lane == idx
        chosen = jnp.where(hit, 1.0, chosen)
        work = jnp.where(hit, -jnp.inf, work)
        picks.append(idx)
        tops.append(top)

    exps = [jnp.exp(t - tops[0]) for t in tops]
    denom = exps[0] + exps[1] + exps[2] + exps[3]

    before = (_iota((tm, tm), 0) > _iota((tm, tm), 1)).astype(BF16)
    ahead = base_ref[...] + _dot(before, chosen.astype(BF16))

    e_out = jnp.zeros((tm, TOP_K), I32)
    g_out = jnp.zeros((tm, TOP_K), F32)
    r_out = jnp.zeros((tm, TOP_K), F32)
    for k in range(TOP_K):
        rank_k = jnp.sum(jnp.where(lane == picks[k], ahead, 0.0), axis=1, keepdims=True)
        e_out = jnp.where(slot == k, picks[k], e_out)
        g_out = jnp.where(slot == k, exps[k] / denom, g_out)
        r_out = jnp.where(slot == k, rank_k, r_out)
    e_ref[...] = e_out
    gate_ref[...] = g_out
    rank_ref[...] = r_out.astype(I32)

    base_ref[...] = base_ref[...] + jnp.sum(chosen, axis=0, keepdims=True)
    cnt_ref[...] = base_ref[...]


def _ln_router(x, upd, g, b, router_w, router_b, tm=256):
    m_rows, d = x.shape
    tm = min(tm, m_rows)
    row = lambda i: (i, 0)
    fixed = lambda i: (0, 0)
    return pl.pallas_call(
        functools.partial(_ln_router_kernel, tm=tm),
        grid=(m_rows // tm,),
        in_specs=[pl.BlockSpec((tm, d), row), pl.BlockSpec((tm, d), row),
                  pl.BlockSpec((1, d), fixed), pl.BlockSpec((1, d), fixed),
                  pl.BlockSpec((d, N_EXPERTS), fixed), pl.BlockSpec((1, N_EXPERTS), fixed)],
        out_specs=[pl.BlockSpec((tm, d), row), pl.BlockSpec((tm, TOP_K), row),
                   pl.BlockSpec((tm, TOP_K), row), pl.BlockSpec((tm, TOP_K), row),
                   pl.BlockSpec((1, N_EXPERTS), fixed)],
        out_shape=[jax.ShapeDtypeStruct((m_rows, d), F32),
                   jax.ShapeDtypeStruct((m_rows, TOP_K), I32),
                   jax.ShapeDtypeStruct((m_rows, TOP_K), F32),
                   jax.ShapeDtypeStruct((m_rows, TOP_K), I32),
                   jax.ShapeDtypeStruct((1, N_EXPERTS), F32)],
        scratch_shapes=[pltpu.VMEM((1, N_EXPERTS), F32)],
        compiler_params=_params(1),
    )(x, upd, g.reshape(1, d), b.reshape(1, d), router_w, router_b.reshape(1, N_EXPERTS))


def _row_copy(src_hbm, row, dst, sem):
    return pltpu.make_async_copy(src_hbm.at[pl.ds(row, 1), :], dst, sem)


def _dispatch_kernel(tok_ref, meta_ref, x_hbm, o_ref, buf, sem, *, rb):
    i = pl.program_id(0)
    n_active = meta_ref[0]

    def start(blk, slot):
        for r in range(rb):
            _row_copy(x_hbm, tok_ref[blk * rb + r], buf.at[slot, pl.ds(r, 1), :], sem.at[slot]).start()

    @pl.when(i == 0)
    def _():
        start(0, 0)

    for slot in range(2):
        @pl.when(jnp.logical_and(i + 1 < n_active, (i + 1) % 2 == slot))
        def _():
            start(i + 1, slot)

    @pl.when(i < n_active)
    def _():
        slot = i % 2
        pltpu.make_async_copy(x_hbm.at[pl.ds(0, rb), :], buf.at[slot], sem.at[slot]).wait()
        o_ref[...] = buf[slot].astype(BF16)

    @pl.when(i >= n_active)
    def _():
        o_ref[...] = jnp.zeros_like(o_ref)


def _dispatch(x, row_tok, meta, rb):
    m_rows, d = x.shape
    n_rows = row_tok.shape[0]
    return pl.pallas_call(
        functools.partial(_dispatch_kernel, rb=rb),
        grid_spec=pltpu.PrefetchScalarGridSpec(
            num_scalar_prefetch=2,
            grid=(n_rows // rb,),
            in_specs=[pl.BlockSpec(memory_space=pl.ANY)],
            out_specs=pl.BlockSpec((rb, d), lambda i, tok, meta: (i, 0)),
            scratch_shapes=[pltpu.VMEM((2, rb, d), F32), pltpu.SemaphoreType.DMA((2,))]),
        out_shape=jax.ShapeDtypeStruct((n_rows, d), BF16),
        compiler_params=_params(1),
    )(row_tok, meta, x)


def _expert_weights(be_ref, nxt_ref, gidx_ref, meta_ref, w_hbm, wbuf, sem, *, layer, w_cols):
    n = pl.program_id(0)
    m = pl.program_id(1)
    n_tiles = pl.num_programs(0)
    n_active, n_groups = meta_ref[0], meta_ref[1]
    active = m < n_active
    fresh = jnp.logical_and(active, jnp.logical_or(m == 0, be_ref[m] != be_ref[jnp.maximum(m - 1, 0)]))
    half = (n * n_groups + gidx_ref[m]) % 2

    def tile(e, col, h):
        cols = pl.ds(pl.multiple_of(col * w_cols, w_cols), w_cols)
        return pltpu.make_async_copy(w_hbm.at[layer, e, :, cols], wbuf.at[h], sem.at[h])

    @pl.when(jnp.logical_and(fresh, jnp.logical_and(n == 0, m == 0)))
    def _():
        tile(be_ref[0], 0, 0).start()

    @pl.when(fresh)
    def _():
        nxt_col = jnp.where(gidx_ref[m] == n_groups - 1, n + 1, n)

        @pl.when(nxt_col < n_tiles)
        def _():
            tile(nxt_ref[m], nxt_col, 1 - half).start()

        tile(be_ref[m], n, half).wait()

    return active, fresh, half


def _gate_up_kernel(be_ref, nxt_ref, gidx_ref, meta_ref, a_ref, w_hbm, bg_ref, bl_ref, o_ref,
                    wbuf, sem, wg_s, wl_s, *, tf, layer):
    active, fresh, half = _expert_weights(be_ref, nxt_ref, gidx_ref, meta_ref, w_hbm, wbuf, sem,
                                          layer=layer, w_cols=2 * tf)
    grp = 2 * HEAD_DIM

    @pl.when(fresh)
    def _():
        r = _iota((grp, grp), 0)
        c = _iota((grp, grp), 1)
        source = jnp.where(c < HEAD_DIM, 2 * c, 2 * (c - HEAD_DIM) + 1)
        pick = (r == source).astype(BF16)
        for g in range(2 * tf // grp):
            wb = wbuf[half, :, g * grp:(g + 1) * grp].astype(BF16)
            sel = _dot(wb, pick)
            wg_s[:, g * HEAD_DIM:(g + 1) * HEAD_DIM] = sel[:, :HEAD_DIM].astype(BF16)
            wl_s[:, g * HEAD_DIM:(g + 1) * HEAD_DIM] = sel[:, HEAD_DIM:].astype(BF16)

    @pl.when(active)
    def _():
        a = a_ref[...]
        glu = jnp.minimum(_dot(a, wg_s[...]) + bg_ref[0], SWIGLU_LIMIT)
        lin = jnp.clip(_dot(a, wl_s[...]) + bl_ref[0], -SWIGLU_LIMIT, SWIGLU_LIMIT)
        act = glu * jax.nn.sigmoid(SWIGLU_ALPHA * glu) * (lin + 1.0)
        o_ref[...] = act.astype(o_ref.dtype)

    @pl.when(jnp.logical_not(active))
    def _():
        o_ref[...] = jnp.zeros_like(o_ref)


def _down_kernel(be_ref, nxt_ref, gidx_ref, meta_ref, a_ref, w_hbm, b_ref, o_ref, wbuf, sem, w_s,
                 *, tn, layer):
    active, fresh, half = _expert_weights(be_ref, nxt_ref, gidx_ref, meta_ref, w_hbm, wbuf, sem,
                                          layer=layer, w_cols=tn)

    @pl.when(fresh)
    def _():
        w_s[...] = wbuf[half].astype(BF16)

    @pl.when(active)
    def _():
        o_ref[...] = (_dot(a_ref[...], w_s[...]) + b_ref[0]).astype(o_ref.dtype)

    @pl.when(jnp.logical_not(active))
    def _():
        o_ref[...] = jnp.zeros_like(o_ref)


def _grouped_specs(rb, k_dim, b_cols, n_bias):
    def m_eff(m, meta):
        return jnp.minimum(m, meta[0] - 1)
    a_spec = pl.BlockSpec((rb, k_dim), lambda n, m, be, nxt, gidx, meta: (m_eff(m, meta), 0))
    w_spec = pl.BlockSpec(memory_space=pl.ANY)
    b_spec = pl.BlockSpec((1, 1, b_cols), lambda n, m, be, nxt, gidx, meta: (be[m_eff(m, meta)], 0, n))
    o_spec = pl.BlockSpec((rb, b_cols), lambda n, m, be, nxt, gidx, meta: (m, n))
    return [a_spec, w_spec] + [b_spec] * n_bias, o_spec


def _moe_gate_up(xs, w_gu, layer, b_glu, b_lin, tables, rb, tf=1024):
    n_rows, d = xs.shape
    ff = w_gu.shape[3] // 2
    in_specs, o_spec = _grouped_specs(rb, d, tf, 2)
    return pl.pallas_call(
        functools.partial(_gate_up_kernel, tf=tf, layer=layer),
        grid_spec=pltpu.PrefetchScalarGridSpec(
            num_scalar_prefetch=4, grid=(ff // tf, n_rows // rb),
            in_specs=in_specs, out_specs=o_spec,
            scratch_shapes=[pltpu.VMEM((2, d, 2 * tf), F32), pltpu.SemaphoreType.DMA((2,)),
                            pltpu.VMEM((d, tf), BF16), pltpu.VMEM((d, tf), BF16)]),
        out_shape=jax.ShapeDtypeStruct((n_rows, ff), BF16),
        compiler_params=_params(2),
    )(*tables, xs, w_gu, b_glu, b_lin)


def _moe_down(act, w_down, layer, b_down, tables, rb, tn=2048):
    n_rows, ff = act.shape
    d = w_down.shape[3]
    in_specs, o_spec = _grouped_specs(rb, ff, tn, 1)
    return pl.pallas_call(
        functools.partial(_down_kernel, tn=tn, layer=layer),
        grid_spec=pltpu.PrefetchScalarGridSpec(
            num_scalar_prefetch=4, grid=(d // tn, n_rows // rb),
            in_specs=in_specs, out_specs=o_spec,
            scratch_shapes=[pltpu.VMEM((2, ff, tn), F32), pltpu.SemaphoreType.DMA((2,)),
                            pltpu.VMEM((ff, tn), BF16)]),
        out_shape=jax.ShapeDtypeStruct((n_rows, d), F32),
        compiler_params=_params(2),
    )(*tables, act, w_down, b_down)


def _combine_kernel(dest_ref, y_hbm, x_ref, gate_ref, g_ref, b_ref, *rest, tm, with_bf16):
    if with_bf16:
        xo_ref, xb_ref, buf, sem = rest
    else:
        xo_ref, buf, sem = rest
    i = pl.program_id(0)
    nb = pl.num_programs(0)
    n_slots = tm * TOP_K

    def start(blk, slot):
        for tok in range(tm):
            for k in range(TOP_K):
                row = dest_ref[blk * n_slots + tok * TOP_K + k]
                _row_copy(y_hbm, row, buf.at[slot, k, pl.ds(tok, 1), :], sem.at[slot]).start()

    @pl.when(i == 0)
    def _():
        start(0, 0)

    for slot in range(2):
        @pl.when(jnp.logical_and(i + 1 < nb, (i + 1) % 2 == slot))
        def _():
            start(i + 1, slot)

    for k in range(TOP_K):
        pltpu.make_async_copy(y_hbm.at[pl.ds(0, tm), :], buf.at[i % 2, k], sem.at[i % 2]).wait()
    gate = gate_ref[...]
    cur = buf.at[i % 2]
    ffn = gate[:, 0:1] * cur[0]
    for k in range(1, TOP_K):
        ffn = ffn + gate[:, k:k + 1] * cur[k]
    y = _deepnorm_ln(x_ref[...], ffn, g_ref[...], b_ref[...])
    xo_ref[...] = y
    if with_bf16:
        xb_ref[...] = y.astype(BF16)


def _combine(yb, dest, x, gate, g, b, with_bf16, tm=128):
    m_rows, d = x.shape
    tm = min(tm, m_rows)
    row = lambda i, dst: (i, 0)
    fixed = lambda i, dst: (0, 0)
    out_specs = [pl.BlockSpec((tm, d), row)]
    out_shape = [jax.ShapeDtypeStruct((m_rows, d), F32)]
    if with_bf16:
        out_specs.append(pl.BlockSpec((tm, d), row))
        out_shape.append(jax.ShapeDtypeStruct((m_rows, d), BF16))
    return pl.pallas_call(
        functools.partial(_combine_kernel, tm=tm, with_bf16=with_bf16),
        grid_spec=pltpu.PrefetchScalarGridSpec(
            num_scalar_prefetch=1, grid=(m_rows // tm,),
            in_specs=[pl.BlockSpec(memory_space=pl.ANY), pl.BlockSpec((tm, d), row),
                      pl.BlockSpec((tm, TOP_K), row), pl.BlockSpec((1, d), fixed), pl.BlockSpec((1, d), fixed)],
            out_specs=out_specs,
            scratch_shapes=[pltpu.VMEM((2, TOP_K, tm, d), F32), pltpu.SemaphoreType.DMA((2,))]),
        out_shape=out_shape,
        compiler_params=_params(1),
    )(dest, yb, x, gate, g.reshape(1, d), b.reshape(1, d))


def _moe_layer(x1, e_idx, gate, rank, counts, w_gu, b_gu, w_down, b_down, layer, ln_g, ln_b, with_bf16):
    m_rows, d = x1.shape
    rb = MOE_ROWS
    n_slots = m_rows * TOP_K
    n_blocks = -(-n_slots // rb) + N_EXPERTS
    n_rows = n_blocks * rb
    cnt = counts.reshape(N_EXPERTS).astype(I32)
    blocks_e = (cnt + rb - 1) // rb
    blk_end = jnp.cumsum(blocks_e)
    row_start = (blk_end - blocks_e) * rb
    dest = (row_start[e_idx] + rank).reshape(n_slots)
    blk = jnp.arange(n_blocks, dtype=I32)
    block_e = jnp.minimum(jnp.sum((blk_end[None, :] <= blk[:, None]).astype(I32), axis=1), N_EXPERTS - 1)
    ids = jnp.arange(N_EXPERTS, dtype=I32)
    owns = blocks_e > 0
    group_of_e = jnp.cumsum(owns.astype(I32)) - 1
    later = jnp.where(jnp.logical_and(owns[None, :], ids[None, :] > ids[:, None]), ids[None, :], N_EXPERTS)
    first = jnp.min(jnp.where(owns, ids, N_EXPERTS))
    next_e = jnp.min(later, axis=1)
    next_e = jnp.where(next_e == N_EXPERTS, first, next_e)
    meta = jnp.stack([blk_end[-1], group_of_e[-1] + 1]).astype(I32)
    tables = (block_e, next_e[block_e].astype(I32), group_of_e[block_e].astype(I32), meta)
    slot_tok = jnp.arange(n_slots, dtype=I32) // TOP_K
    row_tok = jnp.zeros((n_rows,), I32).at[dest].set(slot_tok)

    ff = w_down.shape[2]
    b_pairs = b_gu.reshape(N_EXPERTS, ff, 2)
    b_glu = b_pairs[:, :, 0].reshape(N_EXPERTS, 1, ff)
    b_lin = b_pairs[:, :, 1].reshape(N_EXPERTS, 1, ff)

    xs = _dispatch(x1, row_tok, meta, rb)
    act = _moe_gate_up(xs, w_gu, layer, b_glu, b_lin, tables, rb)
    yb = _moe_down(act, w_down, layer, b_down.reshape(N_EXPERTS, 1, d), tables, rb)
    return _combine(yb, dest, x1, gate, ln_g, ln_b, with_bf16)


def _conv_kernel(x_ref, halo_ref, w_ref, o_ref, ext, *, tt, nt, n_key_blocks, heads_per_block):
    i = pl.program_id(0)
    c = pl.program_id(1)
    first = (i % nt) == 0
    x = x_ref[...].astype(F32)
    ext[0:CONV_HALO, :] = jnp.where(first, 0.0, halo_ref[...].astype(F32))
    ext[CONV_HALO:, :] = x
    w = w_ref[...]
    acc = x * w[GDN_CONV - 1:GDN_CONV, :]
    for j in range(GDN_CONV - 1):
        acc = acc + ext[pl.ds(CONV_HALO - (GDN_CONV - 1) + j, tt), :] * w[j:j + 1, :]
    y = acc * jax.nn.sigmoid(acc)

    def normed(scale):
        for hh in range(heads_per_block):
            seg = y[:, hh * HEAD_DIM:(hh + 1) * HEAD_DIM]
            inv = lax.rsqrt(jnp.sum(seg * seg, axis=1, keepdims=True) + L2_EPS)
            o_ref[:, hh * HEAD_DIM:(hh + 1) * HEAD_DIM] = (seg * (inv * scale)).astype(o_ref.dtype)

    @pl.when(c < n_key_blocks)
    def _():
        normed(HEAD_DIM ** -0.5)

    @pl.when(jnp.logical_and(c >= n_key_blocks, c < 2 * n_key_blocks))
    def _():
        normed(1.0)

    @pl.when(c >= 2 * n_key_blocks)
    def _():
        o_ref[...] = y.astype(o_ref.dtype)


def _gdn_conv(pre, conv_w, batch, seq, tt=256, cb=1024):
    m_rows, ch = pre.shape
    tt = min(tt, seq)
    nt = seq // tt
    key_dim = GDN_K_HEADS * HEAD_DIM
    kernel = functools.partial(_conv_kernel, tt=tt, nt=nt, n_key_blocks=key_dim // cb,
                               heads_per_block=cb // HEAD_DIM)
    return pl.pallas_call(
        kernel,
        grid=(m_rows // tt, ch // cb),
        in_specs=[pl.BlockSpec((tt, cb), lambda i, c: (i, c)),
                  pl.BlockSpec((CONV_HALO, cb), lambda i, c: (jnp.maximum(i * (tt // CONV_HALO) - 1, 0), c)),
                  pl.BlockSpec((GDN_CONV, cb), lambda i, c: (0, c))],
        out_specs=pl.BlockSpec((tt, cb), lambda i, c: (i, c)),
        out_shape=jax.ShapeDtypeStruct((m_rows, ch), BF16),
        scratch_shapes=[pltpu.VMEM((tt + CONV_HALO, cb), F32)],
        compiler_params=_params(2),
    )(pre, pre, conv_w)


def _gates_kernel(ba_ref, alog_ref, dtb_ref, beta_ref, gc_ref, *, tm):
    ba = ba_ref[...]
    beta_ref[...] = jax.nn.sigmoid(ba[:, :GDN_V_HEADS])
    pre = ba[:, GDN_V_HEADS:] + dtb_ref[...]
    softplus = jnp.maximum(pre, 0.0) + jnp.log(1.0 + jnp.exp(-jnp.abs(pre)))
    g = -jnp.exp(alog_ref[...]) * softplus
    r = _iota((tm, tm), 0)
    c = _iota((tm, tm), 1)
    upto = jnp.logical_and(r >= c, r // GDN_CHUNK == c // GDN_CHUNK).astype(BF16)
    g1 = g.astype(BF16)
    rem = g - g1.astype(F32)
    g2 = rem.astype(BF16)
    g3 = (rem - g2.astype(F32)).astype(BF16)
    gc_ref[...] = _dot(upto, g1) + _dot(upto, g2) + _dot(upto, g3)


def _gdn_gates(ba, a_log, dt_bias, tm=256):
    m_rows = ba.shape[0]
    tm = min(tm, m_rows)
    hv = GDN_V_HEADS
    return pl.pallas_call(
        functools.partial(_gates_kernel, tm=tm),
        grid=(m_rows // tm,),
        in_specs=[pl.BlockSpec((tm, 2 * hv), lambda i: (i, 0)),
                  pl.BlockSpec((1, hv), lambda i: (0, 0)), pl.BlockSpec((1, hv), lambda i: (0, 0))],
        out_specs=[pl.BlockSpec((tm, hv), lambda i: (i, 0)), pl.BlockSpec((tm, hv), lambda i: (i, 0))],
        out_shape=[jax.ShapeDtypeStruct((m_rows, hv), F32), jax.ShapeDtypeStruct((m_rows, hv), F32)],
        compiler_params=_params(1),
    )(ba, a_log.reshape(1, hv), dt_bias.reshape(1, hv))


def _gdn_core_kernel(q_ref, k_ref, v_ref, z_ref, beta_ref, gc_ref, nw_ref, o_ref,
                     inject_s, mix_s, local_s, read_s, keep_s, state_s, *, seq):
    hk = pl.program_id(1)
    ch = GDN_CHUNK
    n_chunks = seq // ch
    rep = GDN_V_HEADS // GDN_K_HEADS
    ri = _iota((ch, ch), 0)
    ci = _iota((ch, ch), 1)
    incl, strict, eye = ri >= ci, ri > ci, ri == ci
    head_lane = _iota((ch, GDN_V_HEADS), 1)
    nw = nw_ref[...]

    group = min(GDN_GROUP, n_chunks)

    def local_terms(t, carry):
        probs = []
        prods = []
        for c in range(group):
            n = t * group + c
            rows = pl.ds(pl.multiple_of(n * ch, ch), ch)
            q = q_ref[rows, :].astype(F32)
            k = k_ref[rows, :].astype(F32)
            beta_all = beta_ref[rows, :]
            gc_all = gc_ref[rows, :]
            kbetas = []
            for j in range(rep):
                pick = head_lane == (rep * hk + j)
                bj = jnp.sum(jnp.where(pick, beta_all, 0.0), axis=1, keepdims=True)
                gj = jnp.sum(jnp.where(pick, gc_all, 0.0), axis=1, keepdims=True)
                grow = jnp.sum(jnp.where(eye, gj, 0.0), axis=0, keepdims=True)
                dj = jnp.where(incl, jnp.exp(jnp.where(incl, gj - grow, 0.0)), 0.0)
                kbetas.append(k * bj)
                probs.append(dict(c=c, j=j, n=n, rows=rows, q=q, k=k, bj=bj, gj=gj, dj=dj, kbeta=kbetas[-1]))
            stacked = jnp.concatenate([kb.astype(BF16) for kb in kbetas] + [q.astype(BF16)], axis=0)
            prods.append(_dot_nt(stacked, k.astype(BF16)))
        for p in probs:
            pr = prods[p["c"]]
            j = p["j"]
            p["pm"] = -jnp.where(strict, pr[j * ch:(j + 1) * ch, :] * p["dj"], 0.0)
            p["qk"] = jnp.where(incl, pr[rep * ch:, :] * p["dj"], 0.0)
        for p in probs:
            n16 = p["pm"].astype(BF16)
            p["power"] = _dot(n16, n16)
        for step in range(5):
            for p in probs:
                p16 = p["power"].astype(BF16)
                if step < 4:
                    both = _dot(jnp.concatenate([p16, p["pm"].astype(BF16)], axis=0), p16)
                    p["pm"] = p["pm"] + p["power"] + both[ch:, :]
                    p["power"] = both[:ch, :]
                else:
                    p["pm"] = p["pm"] + p["power"] + _dot(p["pm"].astype(BF16), p16)
        for p in probs:
            j, gj = p["j"], p["gj"]
            p["eg"] = jnp.exp(gj)
            v = v_ref[p["rows"], j * HEAD_DIM:(j + 1) * HEAD_DIM].astype(F32)
            rhs = jnp.concatenate([v * p["bj"], p["kbeta"] * p["eg"]], axis=1)
            p["uw"] = rhs + _dot(p["pm"].astype(BF16), rhs.astype(BF16))
        for p in probs:
            j, n, gj = p["j"], p["n"], p["gj"]
            g_last = gj[ch - 1:ch, :]
            k_dec = p["k"] * jnp.exp(g_last - gj)
            lhs = jnp.concatenate([k_dec.T.astype(BF16), p["qk"].astype(BF16)], axis=0)
            big = _dot(lhs, p["uw"].astype(BF16))
            inject_s[j, n] = big[:HEAD_DIM, :HEAD_DIM]
            mix_s[j, n] = big[:HEAD_DIM, HEAD_DIM:].astype(BF16)
            local_s[j, p["rows"], :] = big[HEAD_DIM:, :HEAD_DIM]
            read_s[j, p["rows"], :] = (p["q"] * p["eg"] - big[HEAD_DIM:, HEAD_DIM:]).astype(BF16)
            keep_s[j, n] = jnp.broadcast_to(jnp.exp(g_last), (1, HEAD_DIM))
        return carry

    lax.fori_loop(0, n_chunks // group, local_terms, 0)

    def advance(n, states):
        nxt = []
        for j in range(rep):
            s16 = states[j].astype(BF16)
            state_s[j, n] = s16
            nxt.append(keep_s[j, n] * states[j] - _dot(mix_s[j, n], s16) + inject_s[j, n])
        return tuple(nxt)

    zero = jnp.zeros((HEAD_DIM, HEAD_DIM), F32)
    lax.fori_loop(0, n_chunks, advance, (zero,) * rep)

    def outputs(n, carry):
        rows = pl.ds(pl.multiple_of(n * ch, ch), ch)
        for j in range(rep):
            o = _dot(read_s[j, rows, :], state_s[j, n]) + local_s[j, rows, :]
            z = z_ref[rows, j * HEAD_DIM:(j + 1) * HEAD_DIM].astype(F32)
            inv = lax.rsqrt(jnp.mean(o * o, axis=1, keepdims=True) + RMS_EPS)
            o_ref[rows, j * HEAD_DIM:(j + 1) * HEAD_DIM] = (
                o * inv * nw * (z * jax.nn.sigmoid(z))).astype(o_ref.dtype)
        return carry

    lax.fori_loop(0, n_chunks, outputs, 0, unroll=4)


def _gdn_core(qkv, z, beta, gc, norm_w, batch, seq):
    hk, hv = GDN_K_HEADS, GDN_V_HEADS
    rep = hv // hk
    wide = rep * HEAD_DIM
    v_off = 2 * hk * HEAD_DIM // wide
    n_chunks = seq // GDN_CHUNK
    return pl.pallas_call(
        functools.partial(_gdn_core_kernel, seq=seq),
        grid=(batch, hk),
        in_specs=[pl.BlockSpec((seq, HEAD_DIM), lambda b, h: (b, h)),
                  pl.BlockSpec((seq, HEAD_DIM), lambda b, h: (b, hk + h)),
                  pl.BlockSpec((seq, wide), lambda b, h: (b, v_off + h)),
                  pl.BlockSpec((seq, wide), lambda b, h: (b, h)),
                  pl.BlockSpec((seq, hv), lambda b, h: (b, 0)),
                  pl.BlockSpec((seq, hv), lambda b, h: (b, 0)),
                  pl.BlockSpec((1, HEAD_DIM), lambda b, h: (0, 0))],
        out_specs=pl.BlockSpec((seq, wide), lambda b, h: (b, h)),
        out_shape=jax.ShapeDtypeStruct((batch * seq, hv * HEAD_DIM), BF16),
        scratch_shapes=[pltpu.VMEM((rep, n_chunks, HEAD_DIM, HEAD_DIM), F32),
                        pltpu.VMEM((rep, n_chunks, HEAD_DIM, HEAD_DIM), BF16),
                        pltpu.VMEM((rep, seq, HEAD_DIM), F32),
                        pltpu.VMEM((rep, seq, HEAD_DIM), BF16),
                        pltpu.VMEM((rep, n_chunks, 1, HEAD_DIM), F32),
                        pltpu.VMEM((rep, n_chunks, HEAD_DIM, HEAD_DIM), BF16)],
        compiler_params=_params(2),
    )(qkv, qkv, qkv, z, beta, gc, norm_w.reshape(1, HEAD_DIM))


def _gated_deltanet(xb, w_in, conv_w, a_log, dt_bias, norm_w, w_o, j, batch, seq):
    key_dim = GDN_K_HEADS * HEAD_DIM
    val_dim = GDN_V_HEADS * HEAD_DIM
    conv_ch = 2 * key_dim + val_dim
    pre = _dense_matmul(xb, w_in, lead=(j,), col0=0, n_cols=conv_ch, out_dtype=BF16)
    z = _dense_matmul(xb, w_in, lead=(j,), col0=conv_ch, n_cols=val_dim, out_dtype=BF16)
    ba = _dense_matmul(xb, w_in[j, :, conv_ch + val_dim:])
    qkv = _gdn_conv(pre, conv_w, batch, seq)
    beta, gc = _gdn_gates(ba, a_log, dt_bias)
    og = _gdn_core(qkv, z, beta, gc, norm_w, batch, seq)
    return _dense_matmul(og, w_o, lead=(j,), tn=512)


def kernel(x, sb_w_qkv, sb_w_o, gdn_w_in, gdn_conv_w, gdn_a_log, gdn_dt_bias, gdn_norm_w, gdn_w_o, ln_mix_g, ln_mix_b, ln_ffn_g, ln_ffn_b, moe_router_w, moe_router_b, moe_w_gu, moe_b_gu, moe_w_down, moe_b_down):
    batch, seq, d = x.shape
    xf = x.reshape(batch * seq, d)
    xb = xf.astype(BF16)
    for i in range(DEPTH):
        j = i // 2
        if i % 2 == 0:
            qkv = _dense_matmul(xb, sb_w_qkv, lead=(j,), out_dtype=BF16)
            attn = _sb_attention(qkv, batch, seq)
            mix = _dense_matmul(attn, sb_w_o, lead=(j,))
        else:
            mix = _gated_deltanet(xb, gdn_w_in, gdn_conv_w[j], gdn_a_log[j], gdn_dt_bias[j],
                                  gdn_norm_w[j], gdn_w_o, j, batch, seq)
        x1, e_idx, gate, rank, counts = _ln_router(xf, mix, ln_mix_g[i], ln_mix_b[i],
                                                   moe_router_w[i], moe_router_b[i])
        last = i == DEPTH - 1
        outs = _moe_layer(x1, e_idx, gate, rank, counts, moe_w_gu, moe_b_gu[i], moe_w_down,
                          moe_b_down[i], i, ln_ffn_g[i], ln_ffn_b[i], with_bf16=not last)
        xf = outs[0]
        if not last:
            xb = outs[1]
    return xf.reshape(batch, seq, d)
```

```python
import functools

import jax
import jax.numpy as jnp
from jax import lax
from jax.experimental import pallas as pl
from jax.experimental.pallas import tpu as pltpu

F32, BF16, I32 = jnp.float32, jnp.bfloat16, jnp.int32

DEPTH = 2
N_EXPERTS = 32
TOP_K = 4
HEAD_DIM = 128
LANES = 128
SB_HEADS = 16
GDN_K_HEADS = 16
GDN_V_HEADS = 32
GDN_CONV = 4
GDN_CHUNK = 64
SWIGLU_LIMIT = 7.0
SWIGLU_ALPHA = 1.702
DEEPNORM_ALPHA = (2 * DEPTH) ** 0.25
LN_EPS = 1e-5
RMS_EPS = 1e-6
L2_EPS = 1e-6

VMEM_LIMIT_BYTES = 56 * 1024 * 1024
GDN_KEY_HEADS_PER_STEP = 2
GDN_GROUP = 8
CONV_HALO = 16
MOE_ROWS = 256
ATTN_BLOCK = 256
ATTN_SPAN = 4


def _params(n_axes):
    return pltpu.CompilerParams(dimension_semantics=("arbitrary",) * n_axes,
                                vmem_limit_bytes=VMEM_LIMIT_BYTES)


def _iota(shape, dim):
    return lax.broadcasted_iota(I32, shape, dim)


def _split_bf16(x):
    hi = x.astype(BF16)
    lo = (x - hi.astype(F32)).astype(BF16)
    return hi, lo


def _dot(a, b):
    return jnp.dot(a, b, preferred_element_type=F32)


def _dot_nt(a, b):
    return lax.dot_general(a, b, (((1,), (1,)), ((), ())), preferred_element_type=F32)


def _mm_kernel(a_ref, w_ref, o_ref, w_s):
    @pl.when(pl.program_id(1) == 0)
    def _():
        w_s[...] = w_ref[...].astype(BF16)

    o_ref[...] = _dot(a_ref[...], w_s[...]).astype(o_ref.dtype)


def _dense_matmul(a, w, *, lead=(), col0=0, n_cols=None, out_dtype=F32, tm=512, tn=1024):
    m_rows, k_dim = a.shape
    n_cols = w.shape[-1] if n_cols is None else n_cols
    tn = min(tn, n_cols)
    tm = min(tm, m_rows)
    assert n_cols % tn == 0 and col0 % tn == 0 and m_rows % tm == 0
    off = col0 // tn
    assert len(lead) == w.ndim - 2
    return pl.pallas_call(
        _mm_kernel,
        grid=(n_cols // tn, m_rows // tm),
        in_specs=[pl.BlockSpec((tm, k_dim), lambda n, m: (m, 0)),
                  pl.BlockSpec((None,) * len(lead) + (k_dim, tn), lambda n, m: lead + (0, n + off))],
        out_specs=pl.BlockSpec((tm, tn), lambda n, m: (m, n)),
        out_shape=jax.ShapeDtypeStruct((m_rows, n_cols), out_dtype),
        scratch_shapes=[pltpu.VMEM((k_dim, tn), BF16)],
        compiler_params=_params(2),
    )(a, w)


def _sb_attn_kernel(q_ref, k_ref, v_ref, o_ref, *, blk, span, scale):
    i = pl.program_id(2)
    q = q_ref[...]
    rows = _iota((blk, blk), 0)
    cols = _iota((blk, blk), 1)
    later = (rows > cols).astype(BF16)
    causal = cols < rows

    def visit(blocks, carry, acc, diagonal):
        n = len(blocks)
        ks = [k_ref[pl.ds(pl.multiple_of(j * blk, blk), blk), :] for j in blocks]
        vs = [v_ref[pl.ds(pl.multiple_of(j * blk, blk), blk), :] for j in blocks]
        zs = [_dot_nt(q, kb) * scale for kb in ks]
        log_beta, log_1m, parts = [], [], []
        for idx, z in enumerate(zs):
            lb = jnp.minimum(z, 0.0) - jnp.log(1.0 + jnp.exp(-jnp.abs(z)))
            lm = lb - z
            if diagonal and idx == 0:
                lm = jnp.where(causal, lm, 0.0)
            log_beta.append(lb)
            log_1m.append(lm)
            parts.extend(_split_bf16(lm))
        sums = _dot(jnp.concatenate(parts, axis=0), later)
        for idx in range(n):
            after = sums[2 * idx * blk:(2 * idx + 1) * blk] + sums[(2 * idx + 1) * blk:(2 * idx + 2) * blk]
            w = jnp.exp(log_beta[idx] + after + carry)
            if diagonal and idx == 0:
                w = jnp.where(causal, w, 0.0)
            acc = acc + _dot(w.astype(BF16), vs[idx])
            carry = carry + jnp.sum(log_1m[idx], axis=1, keepdims=True)
        return carry, acc

    start = (jnp.zeros((blk, 1), F32), jnp.zeros((blk, HEAD_DIM), F32))
    extra = i % span
    first = [functools.partial(visit, [i - s for s in range(n + 1)], *start, True) for n in range(span)]
    carry, acc = lax.switch(extra, first)

    def body(t, state):
        j = i - extra - 1 - span * t
        return visit([j - s for s in range(span)], *state, False)

    carry, acc = lax.fori_loop(0, (i - extra) // span, body, (carry, acc))
    o_ref[...] = acc.astype(o_ref.dtype)


def _sb_attention(qkv, batch, seq):
    blk = min(ATTN_BLOCK, seq)
    nq = seq // blk
    h = SB_HEADS
    kernel = functools.partial(_sb_attn_kernel, blk=blk, span=min(ATTN_SPAN, nq), scale=HEAD_DIM ** -0.5)
    return pl.pallas_call(
        kernel,
        grid=(batch, h, nq),
        in_specs=[pl.BlockSpec((blk, HEAD_DIM), lambda b, hh, i: (b * nq + i, hh)),
                  pl.BlockSpec((seq, HEAD_DIM), lambda b, hh, i: (b, h + hh)),
                  pl.BlockSpec((seq, HEAD_DIM), lambda b, hh, i: (b, 2 * h + hh))],
        out_specs=pl.BlockSpec((blk, HEAD_DIM), lambda b, hh, i: (b * nq + i, hh)),
        out_shape=jax.ShapeDtypeStruct((batch * seq, h * HEAD_DIM), BF16),
        compiler_params=_params(3),
    )(qkv, qkv, qkv)


def _deepnorm_ln(x, upd, g, b):
    h = DEEPNORM_ALPHA * x + upd
    mu = jnp.mean(h, axis=-1, keepdims=True)
    hc = h - mu
    var = jnp.mean(hc * hc, axis=-1, keepdims=True)
    return hc * lax.rsqrt(var + LN_EPS) * g + b


def _ln_router_kernel(x_ref, u_ref, g_ref, b_ref, rw_ref, rb_ref,
                      xo_ref, slab_ref, e_ref, gate_ref, rank_ref, cnt_ref, base_ref, *, tm):
    i = pl.program_id(0)

    @pl.when(i == 0)
    def _():
        base_ref[...] = jnp.zeros_like(base_ref)

    y = _deepnorm_ln(x_ref[...], u_ref[...], g_ref[...], b_ref[...])
    xo_ref[...] = y
    n_fold = y.shape[1] // LANES
    for c in range(n_fold):
        slab_ref[pl.ds(c, tm, stride=n_fold), :] = y[:, c * LANES:(c + 1) * LANES]

    yh, yl = _split_bf16(y)
    wh, wl = _split_bf16(rw_ref[...])
    logits = _dot(yh, wh) + _dot(yl, wh) + _dot(yh, wl) + rb_ref[...]

    lane = _iota((tm, N_EXPERTS), 1)
    slot = _iota((tm, TOP_K), 1)
    work = logits
    picks, tops = [], []
    chosen = jnp.zeros((tm, N_EXPERTS), F32)
    for _ in range(TOP_K):
        top = jnp.max(work, axis=1, keepdims=True)
        idx = jnp.min(jnp.where(work == top, lane, N_EXPERTS), axis=1, keepdims=True)
        hit = lane == idx
        chosen = jnp.where(hit, 1.0, chosen)
        work = jnp.where(hit, -jnp.inf, work)
        picks.append(idx)
        tops.append(top)

    exps = [jnp.exp(t - tops[0]) for t in tops]
    denom = exps[0] + exps[1] + exps[2] + exps[3]

    before = (_iota((tm, tm), 0) > _iota((tm, tm), 1)).astype(BF16)
    ahead = base_ref[...] + _dot(before, chosen.astype(BF16))

    e_out = jnp.zeros((tm, TOP_K), I32)
    g_out = jnp.zeros((tm, TOP_K), F32)
    r_out = jnp.zeros((tm, TOP_K), F32)
    for k in range(TOP_K):
        rank_k = jnp.sum(jnp.where(lane == picks[k], ahead, 0.0), axis=1, keepdims=True)
        e_out = jnp.where(slot == k, picks[k], e_out)
        g_out = jnp.where(slot == k, exps[k] / denom, g_out)
        r_out = jnp.where(slot == k, rank_k, r_out)
    e_ref[...] = e_out
    gate_ref[...] = g_out
    rank_ref[...] = r_out.astype(I32)

    base_ref[...] = base_ref[...] + jnp.sum(chosen, axis=0, keepdims=True)
    cnt_ref[...] = base_ref[...]


def _ln_router(x, upd, g, b, router_w, router_b, tm=256):
    m_rows, d = x.shape
    tm = min(tm, m_rows)
    row = lambda i: (i, 0)
    fixed = lambda i: (0, 0)
    return pl.pallas_call(
        functools.partial(_ln_router_kernel, tm=tm),
        grid=(m_rows // tm,),
        in_specs=[pl.BlockSpec((tm, d), row), pl.BlockSpec((tm, d), row),
                  pl.BlockSpec((1, d), fixed), pl.BlockSpec((1, d), fixed),
                  pl.BlockSpec((d, N_EXPERTS), fixed), pl.BlockSpec((1, N_EXPERTS), fixed)],
        out_specs=[pl.BlockSpec((tm, d), row), pl.BlockSpec((tm * (d // LANES), LANES), row),
                   pl.BlockSpec((tm, TOP_K), row),
                   pl.BlockSpec((tm, TOP_K), row), pl.BlockSpec((tm, TOP_K), row),
                   pl.BlockSpec((1, N_EXPERTS), fixed)],
        out_shape=[jax.ShapeDtypeStruct((m_rows, d), F32),
                   jax.ShapeDtypeStruct((m_rows * (d // LANES), LANES), F32),
                   jax.ShapeDtypeStruct((m_rows, TOP_K), I32),
                   jax.ShapeDtypeStruct((m_rows, TOP_K), F32),
                   jax.ShapeDtypeStruct((m_rows, TOP_K), I32),
                   jax.ShapeDtypeStruct((1, N_EXPERTS), F32)],
        scratch_shapes=[pltpu.VMEM((1, N_EXPERTS), F32)],
        compiler_params=_params(1),
    )(x, upd, g.reshape(1, d), b.reshape(1, d), router_w, router_b.reshape(1, N_EXPERTS))


def _row_copy(src_hbm, row, dst, sem):
    return pltpu.make_async_copy(src_hbm.at[pl.ds(row, 1), :], dst, sem)


def _dispatch_kernel(tok_ref, meta_ref, x_hbm, o_ref, buf, sem, *, rb, n_fold):
    i = pl.program_id(0)
    n_active = meta_ref[0]

    def start(blk, slot):
        for r in range(rb):
            src = pl.ds(pl.multiple_of(tok_ref[blk * rb + r] * n_fold, n_fold), n_fold)
            pltpu.make_async_copy(x_hbm.at[src, :], buf.at[slot, pl.ds(r * n_fold, n_fold), :],
                                  sem.at[slot]).start()

    @pl.when(i == 0)
    def _():
        start(0, 0)

    for slot in range(2):
        @pl.when(jnp.logical_and(i + 1 < n_active, (i + 1) % 2 == slot))
        def _():
            start(i + 1, slot)

    @pl.when(i < n_active)
    def _():
        slot = i % 2
        pltpu.make_async_copy(x_hbm.at[pl.ds(0, rb * n_fold), :], buf.at[slot], sem.at[slot]).wait()
        for c in range(n_fold):
            o_ref[:, c * LANES:(c + 1) * LANES] = buf[slot, pl.ds(c, rb, stride=n_fold), :].astype(BF16)

    @pl.when(i >= n_active)
    def _():
        o_ref[...] = jnp.zeros_like(o_ref)


def _dispatch(slabs, d, row_tok, meta, rb):
    n_fold = d // LANES
    n_rows = row_tok.shape[0]
    return pl.pallas_call(
        functools.partial(_dispatch_kernel, rb=rb, n_fold=n_fold),
        grid_spec=pltpu.PrefetchScalarGridSpec(
            num_scalar_prefetch=2,
            grid=(n_rows // rb,),
            in_specs=[pl.BlockSpec(memory_space=pl.ANY)],
            out_specs=pl.BlockSpec((rb, d), lambda i, tok, meta: (i, 0)),
            scratch_shapes=[pltpu.VMEM((2, rb * n_fold, LANES), F32), pltpu.SemaphoreType.DMA((2,))]),
        out_shape=jax.ShapeDtypeStruct((n_rows, d), BF16),
        compiler_params=_params(1),
    )(row_tok, meta, slabs)


def _expert_weights(be_ref, nxt_ref, gidx_ref, meta_ref, w_hbm, wbuf, sem, *, layer, w_cols):
    n = pl.program_id(0)
    m = pl.program_id(1)
    n_tiles = pl.num_programs(0)
    n_active, n_groups = meta_ref[0], meta_ref[1]
    active = m < n_active
    fresh = jnp.logical_and(active, jnp.logical_or(m == 0, be_ref[m] != be_ref[jnp.maximum(m - 1, 0)]))
    half = (n * n_groups + gidx_ref[m]) % 2

    def tile(e, col, h):
        cols = pl.ds(pl.multiple_of(col * w_cols, w_cols), w_cols)
        return pltpu.make_async_copy(w_hbm.at[layer, e, :, cols], wbuf.at[h], sem.at[h])

    @pl.when(jnp.logical_and(fresh, jnp.logical_and(n == 0, m == 0)))
    def _():
        tile(be_ref[0], 0, 0).start()

    @pl.when(fresh)
    def _():
        nxt_col = jnp.where(gidx_ref[m] == n_groups - 1, n + 1, n)

        @pl.when(nxt_col < n_tiles)
        def _():
            tile(nxt_ref[m], nxt_col, 1 - half).start()

        tile(be_ref[m], n, half).wait()

    return active, fresh, half


def _emit_rows(active, fill_ref, a_ref, o_ref, compute):
    rb = a_ref.shape[0]
    half_rows = rb // 2
    wide = fill_ref[pl.program_id(1)] > half_rows

    @pl.when(jnp.logical_and(active, wide))
    def _():
        o_ref[...] = compute(a_ref[...]).astype(o_ref.dtype)

    @pl.when(jnp.logical_and(active, jnp.logical_not(wide)))
    def _():
        o_ref[:half_rows, :] = compute(a_ref[:half_rows, :]).astype(o_ref.dtype)
        o_ref[half_rows:, :] = jnp.zeros((rb - half_rows, o_ref.shape[1]), o_ref.dtype)

    @pl.when(jnp.logical_not(active))
    def _():
        o_ref[...] = jnp.zeros_like(o_ref)


def _gate_up_kernel(be_ref, nxt_ref, gidx_ref, fill_ref, meta_ref, a_ref, w_hbm, bg_ref, bl_ref, o_ref,
                    wbuf, sem, wg_s, wl_s, *, tf, layer):
    active, fresh, half = _expert_weights(be_ref, nxt_ref, gidx_ref, meta_ref, w_hbm, wbuf, sem,
                                          layer=layer, w_cols=2 * tf)
    grp = 2 * HEAD_DIM

    @pl.when(fresh)
    def _():
        r = _iota((grp, grp), 0)
        c = _iota((grp, grp), 1)
        source = jnp.where(c < HEAD_DIM, 2 * c, 2 * (c - HEAD_DIM) + 1)
        pick = (r == source).astype(BF16)
        for g in range(2 * tf // grp):
            wb = wbuf[half, :, g * grp:(g + 1) * grp].astype(BF16)
            sel = _dot(wb, pick)
            wg_s[:, g * HEAD_DIM:(g + 1) * HEAD_DIM] = sel[:, :HEAD_DIM].astype(BF16)
            wl_s[:, g * HEAD_DIM:(g + 1) * HEAD_DIM] = sel[:, HEAD_DIM:].astype(BF16)

    def swiglu(a):
        glu = jnp.minimum(_dot(a, wg_s[...]) + bg_ref[0], SWIGLU_LIMIT)
        lin = jnp.clip(_dot(a, wl_s[...]) + bl_ref[0], -SWIGLU_LIMIT, SWIGLU_LIMIT)
        return glu * jax.nn.sigmoid(SWIGLU_ALPHA * glu) * (lin + 1.0)

    _emit_rows(active, fill_ref, a_ref, o_ref, swiglu)


def _down_kernel(be_ref, nxt_ref, gidx_ref, fill_ref, meta_ref, a_ref, w_hbm, b_ref, o_ref, wbuf, sem, w_s,
                 *, tn, layer):
    active, fresh, half = _expert_weights(be_ref, nxt_ref, gidx_ref, meta_ref, w_hbm, wbuf, sem,
                                          layer=layer, w_cols=tn)

    @pl.when(fresh)
    def _():
        w_s[...] = wbuf[half].astype(BF16)

    _emit_rows(active, fill_ref, a_ref, o_ref, lambda a: _dot(a, w_s[...]) + b_ref[0])


def _grouped_specs(rb, k_dim, b_cols, n_bias):
    def m_eff(m, meta):
        return jnp.minimum(m, meta[0] - 1)
    a_spec = pl.BlockSpec((rb, k_dim), lambda n, m, be, nxt, gidx, fill, meta: (m_eff(m, meta), 0))
    w_spec = pl.BlockSpec(memory_space=pl.ANY)
    b_spec = pl.BlockSpec((1, 1, b_cols), lambda n, m, be, nxt, gidx, fill, meta: (be[m_eff(m, meta)], 0, n))
    o_spec = pl.BlockSpec((rb, b_cols), lambda n, m, be, nxt, gidx, fill, meta: (m, n))
    return [a_spec, w_spec] + [b_spec] * n_bias, o_spec


def _moe_gate_up(xs, w_gu, layer, b_glu, b_lin, tables, rb, tf=1024):
    n_rows, d = xs.shape
    ff = w_gu.shape[3] // 2
    in_specs, o_spec = _grouped_specs(rb, d, tf, 2)
    return pl.pallas_call(
        functools.partial(_gate_up_kernel, tf=tf, layer=layer),
        grid_spec=pltpu.PrefetchScalarGridSpec(
            num_scalar_prefetch=5, grid=(ff // tf, n_rows // rb),
            in_specs=in_specs, out_specs=o_spec,
            scratch_shapes=[pltpu.VMEM((2, d, 2 * tf), F32), pltpu.SemaphoreType.DMA((2,)),
                            pltpu.VMEM((d, tf), BF16), pltpu.VMEM((d, tf), BF16)]),
        out_shape=jax.ShapeDtypeStruct((n_rows, ff), BF16),
        compiler_params=_params(2),
    )(*tables, xs, w_gu, b_glu, b_lin)


def _moe_down(act, w_down, layer, b_down, tables, rb, tn=2048):
    n_rows, ff = act.shape
    d = w_down.shape[3]
    in_specs, o_spec = _grouped_specs(rb, ff, tn, 1)
    return pl.pallas_call(
        functools.partial(_down_kernel, tn=tn, layer=layer),
        grid_spec=pltpu.PrefetchScalarGridSpec(
            num_scalar_prefetch=5, grid=(d // tn, n_rows // rb),
            in_specs=in_specs, out_specs=o_spec,
            scratch_shapes=[pltpu.VMEM((2, ff, tn), F32), pltpu.SemaphoreType.DMA((2,)),
                            pltpu.VMEM((ff, tn), BF16)]),
        out_shape=jax.ShapeDtypeStruct((n_rows, d), F32),
        compiler_params=_params(2),
    )(*tables, act, w_down, b_down)


def _combine_kernel(dest_ref, y_hbm, x_ref, gate_ref, g_ref, b_ref, *rest, tm, with_bf16):
    if with_bf16:
        xo_ref, xb_ref, buf, sem = rest
    else:
        xo_ref, buf, sem = rest
    i = pl.program_id(0)
    nb = pl.num_programs(0)
    n_slots = tm * TOP_K

    def start(blk, slot):
        for tok in range(tm):
            for k in range(TOP_K):
                row = dest_ref[blk * n_slots + tok * TOP_K + k]
                _row_copy(y_hbm, row, buf.at[slot, k, pl.ds(tok, 1), :], sem.at[slot]).start()

    @pl.when(i == 0)
    def _():
        start(0, 0)

    for slot in range(2):
        @pl.when(jnp.logical_and(i + 1 < nb, (i + 1) % 2 == slot))
        def _():
            start(i + 1, slot)

    for k in range(TOP_K):
        pltpu.make_async_copy(y_hbm.at[pl.ds(0, tm), :], buf.at[i % 2, k], sem.at[i % 2]).wait()
    gate = gate_ref[...]
    cur = buf.at[i % 2]
    ffn = gate[:, 0:1] * cur[0]
    for k in range(1, TOP_K):
        ffn = ffn + gate[:, k:k + 1] * cur[k]
    y = _deepnorm_ln(x_ref[...], ffn, g_ref[...], b_ref[...])
    xo_ref[...] = y
    if with_bf16:
        xb_ref[...] = y.astype(BF16)


def _combine(yb, dest, x, gate, g, b, with_bf16, tm=128):
    m_rows, d = x.shape
    tm = min(tm, m_rows)
    row = lambda i, dst: (i, 0)
    fixed = lambda i, dst: (0, 0)
    out_specs = [pl.BlockSpec((tm, d), row)]
    out_shape = [jax.ShapeDtypeStruct((m_rows, d), F32)]
    if with_bf16:
        out_specs.append(pl.BlockSpec((tm, d), row))
        out_shape.append(jax.ShapeDtypeStruct((m_rows, d), BF16))
    return pl.pallas_call(
        functools.partial(_combine_kernel, tm=tm, with_bf16=with_bf16),
        grid_spec=pltpu.PrefetchScalarGridSpec(
            num_scalar_prefetch=1, grid=(m_rows // tm,),
            in_specs=[pl.BlockSpec(memory_space=pl.ANY), pl.BlockSpec((tm, d), row),
                      pl.BlockSpec((tm, TOP_K), row), pl.BlockSpec((1, d), fixed), pl.BlockSpec((1, d), fixed)],
            out_specs=out_specs,
            scratch_shapes=[pltpu.VMEM((2, TOP_K, tm, d), F32), pltpu.SemaphoreType.DMA((2,))]),
        out_shape=out_shape,
        compiler_params=_params(1),
    )(dest, yb, x, gate, g.reshape(1, d), b.reshape(1, d))


def _moe_layer(x1, slabs, e_idx, gate, rank, counts, w_gu, b_gu, w_down, b_down, layer, ln_g, ln_b,
               with_bf16):
    m_rows, d = x1.shape
    rb = MOE_ROWS
    n_slots = m_rows * TOP_K
    n_blocks = -(-n_slots // rb) + N_EXPERTS
    n_rows = n_blocks * rb
    cnt = counts.reshape(N_EXPERTS).astype(I32)
    blocks_e = (cnt + rb - 1) // rb
    blk_end = jnp.cumsum(blocks_e)
    row_start = (blk_end - blocks_e) * rb
    dest = (row_start[e_idx] + rank).reshape(n_slots)
    blk = jnp.arange(n_blocks, dtype=I32)
    block_e = jnp.minimum(jnp.sum((blk_end[None, :] <= blk[:, None]).astype(I32), axis=1), N_EXPERTS - 1)
    ids = jnp.arange(N_EXPERTS, dtype=I32)
    owns = blocks_e > 0
    group_of_e = jnp.cumsum(owns.astype(I32)) - 1
    later = jnp.where(jnp.logical_and(owns[None, :], ids[None, :] > ids[:, None]), ids[None, :], N_EXPERTS)
    first = jnp.min(jnp.where(owns, ids, N_EXPERTS))
    next_e = jnp.min(later, axis=1)
    next_e = jnp.where(next_e == N_EXPERTS, first, next_e)
    meta = jnp.stack([blk_end[-1], group_of_e[-1] + 1]).astype(I32)
    first_blk = blk_end - blocks_e
    fill = jnp.clip(cnt[block_e] - (blk - first_blk[block_e]) * rb, 0, rb).astype(I32)
    tables = (block_e, next_e[block_e].astype(I32), group_of_e[block_e].astype(I32), fill, meta)
    slot_tok = jnp.arange(n_slots, dtype=I32) // TOP_K
    row_tok = jnp.zeros((n_rows,), I32).at[dest].set(slot_tok)

    ff = w_down.shape[2]
    b_pairs = b_gu.reshape(N_EXPERTS, ff, 2)
    b_glu = b_pairs[:, :, 0].reshape(N_EXPERTS, 1, ff)
    b_lin = b_pairs[:, :, 1].reshape(N_EXPERTS, 1, ff)

    xs = _dispatch(slabs, d, row_tok, meta, rb)
    act = _moe_gate_up(xs, w_gu, layer, b_glu, b_lin, tables, rb)
    yb = _moe_down(act, w_down, layer, b_down.reshape(N_EXPERTS, 1, d), tables, rb)
    return _combine(yb, dest, x1, gate, ln_g, ln_b, with_bf16)


def _conv_kernel(x_ref, halo_ref, w_ref, o_ref, ext, *, tt, nt, n_key_blocks, heads_per_block):
    i = pl.program_id(0)
    c = pl.program_id(1)
    first = (i % nt) == 0
    x = x_ref[...].astype(F32)
    ext[0:CONV_HALO, :] = jnp.where(first, 0.0, halo_ref[...].astype(F32))
    ext[CONV_HALO:, :] = x
    w = w_ref[...]
    acc = x * w[GDN_CONV - 1:GDN_CONV, :]
    for j in range(GDN_CONV - 1):
        acc = acc + ext[pl.ds(CONV_HALO - (GDN_CONV - 1) + j, tt), :] * w[j:j + 1, :]
    y = acc * jax.nn.sigmoid(acc)

    def normed(scale):
        for hh in range(heads_per_block):
            seg = y[:, hh * HEAD_DIM:(hh + 1) * HEAD_DIM]
            inv = lax.rsqrt(jnp.sum(seg * seg, axis=1, keepdims=True) + L2_EPS)
            o_ref[:, hh * HEAD_DIM:(hh + 1) * HEAD_DIM] = (seg * (inv * scale)).astype(o_ref.dtype)

    @pl.when(c < n_key_blocks)
    def _():
        normed(HEAD_DIM ** -0.5)

    @pl.when(jnp.logical_and(c >= n_key_blocks, c < 2 * n_key_blocks))
    def _():
        normed(1.0)

    @pl.when(c >= 2 * n_key_blocks)
    def _():
        o_ref[...] = y.astype(o_ref.dtype)


def _gdn_conv(pre, conv_w, batch, seq, tt=256, cb=1024):
    m_rows, ch = pre.shape
    tt = min(tt, seq)
    nt = seq // tt
    key_dim = GDN_K_HEADS * HEAD_DIM
    kernel = functools.partial(_conv_kernel, tt=tt, nt=nt, n_key_blocks=key_dim // cb,
                               heads_per_block=cb // HEAD_DIM)
    return pl.pallas_call(
        kernel,
        grid=(m_rows // tt, ch // cb),
        in_specs=[pl.BlockSpec((tt, cb), lambda i, c: (i, c)),
                  pl.BlockSpec((CONV_HALO, cb), lambda i, c: (jnp.maximum(i * (tt // CONV_HALO) - 1, 0), c)),
                  pl.BlockSpec((GDN_CONV, cb), lambda i, c: (0, c))],
        out_specs=pl.BlockSpec((tt, cb), lambda i, c: (i, c)),
        out_shape=jax.ShapeDtypeStruct((m_rows, ch), BF16),
        scratch_shapes=[pltpu.VMEM((tt + CONV_HALO, cb), F32)],
        compiler_params=_params(2),
    )(pre, pre, conv_w)


def _gates_kernel(ba_ref, alog_ref, dtb_ref, beta_ref, gc_ref, *, tm):
    ba = ba_ref[...]
    beta_ref[...] = jax.nn.sigmoid(ba[:, :GDN_V_HEADS])
    pre = ba[:, GDN_V_HEADS:] + dtb_ref[...]
    softplus = jnp.maximum(pre, 0.0) + jnp.log(1.0 + jnp.exp(-jnp.abs(pre)))
    g = -jnp.exp(alog_ref[...]) * softplus
    r = _iota((tm, tm), 0)
    c = _iota((tm, tm), 1)
    upto = jnp.logical_and(r >= c, r // GDN_CHUNK == c // GDN_CHUNK).astype(BF16)
    g1 = g.astype(BF16)
    rem = g - g1.astype(F32)
    g2 = rem.astype(BF16)
    g3 = (rem - g2.astype(F32)).astype(BF16)
    gc_ref[...] = _dot(upto, g1) + _dot(upto, g2) + _dot(upto, g3)


def _gdn_gates(ba, a_log, dt_bias, tm=256):
    m_rows = ba.shape[0]
    tm = min(tm, m_rows)
    hv = GDN_V_HEADS
    return pl.pallas_call(
        functools.partial(_gates_kernel, tm=tm),
        grid=(m_rows // tm,),
        in_specs=[pl.BlockSpec((tm, 2 * hv), lambda i: (i, 0)),
                  pl.BlockSpec((1, hv), lambda i: (0, 0)), pl.BlockSpec((1, hv), lambda i: (0, 0))],
        out_specs=[pl.BlockSpec((tm, hv), lambda i: (i, 0)), pl.BlockSpec((tm, hv), lambda i: (i, 0))],
        out_shape=[jax.ShapeDtypeStruct((m_rows, hv), F32), jax.ShapeDtypeStruct((m_rows, hv), F32)],
        compiler_params=_params(1),
    )(ba, a_log.reshape(1, hv), dt_bias.reshape(1, hv))


def _gdn_core_kernel(q_ref, k_ref, v_ref, z_ref, beta_ref, gc_ref, nw_ref, o_ref,
                     inject_s, mix_s, local_s, read_s, keep_s, state_s, *, seq):
    pair = pl.program_id(1)
    ch = GDN_CHUNK
    n_chunks = seq // ch
    rep = GDN_V_HEADS // GDN_K_HEADS
    kh = q_ref.shape[1] // HEAD_DIM
    heads = kh * rep
    ri = _iota((ch, ch), 0)
    ci = _iota((ch, ch), 1)
    incl, strict, eye = ri >= ci, ri > ci, ri == ci
    head_lane = _iota((ch, GDN_V_HEADS), 1)
    nw = nw_ref[...]

    group = max(1, min(GDN_GROUP // kh, n_chunks))

    def local_terms(t, carry):
        probs = []
        prods = []
        for c, kk in [(c, kk) for c in range(group) for kk in range(kh)]:
            n = t * group + c
            rows = pl.ds(pl.multiple_of(n * ch, ch), ch)
            q = q_ref[rows, kk * HEAD_DIM:(kk + 1) * HEAD_DIM].astype(F32)
            k = k_ref[rows, kk * HEAD_DIM:(kk + 1) * HEAD_DIM].astype(F32)
            beta_all = beta_ref[rows, :]
            gc_all = gc_ref[rows, :]
            kbetas = []
            for j in range(rep):
                h = kk * rep + j
                pick = head_lane == (heads * pair + h)
                bj = jnp.sum(jnp.where(pick, beta_all, 0.0), axis=1, keepdims=True)
                gj = jnp.sum(jnp.where(pick, gc_all, 0.0), axis=1, keepdims=True)
                grow = jnp.sum(jnp.where(eye, gj, 0.0), axis=0, keepdims=True)
                dj = jnp.where(incl, jnp.exp(jnp.where(incl, gj - grow, 0.0)), 0.0)
                kbetas.append(k * bj)
                probs.append(dict(c=len(prods), j=j, h=h, n=n, rows=rows, q=q, k=k, bj=bj, gj=gj, dj=dj,
                                  kbeta=kbetas[-1]))
            stacked = jnp.concatenate([kb.astype(BF16) for kb in kbetas] + [q.astype(BF16)], axis=0)
            prods.append(_dot_nt(stacked, k.astype(BF16)))
        for p in probs:
            pr = prods[p["c"]]
            j = p["j"]
            p["pm"] = -jnp.where(strict, pr[j * ch:(j + 1) * ch, :] * p["dj"], 0.0)
            p["qk"] = jnp.where(incl, pr[rep * ch:, :] * p["dj"], 0.0)
        for p in probs:
            n16 = p["pm"].astype(BF16)
            p["power"] = _dot(n16, n16)
        for step in range(5):
            for p in probs:
                p16 = p["power"].astype(BF16)
                if step < 4:
                    both = _dot(jnp.concatenate([p16, p["pm"].astype(BF16)], axis=0), p16)
                    p["pm"] = p["pm"] + p["power"] + both[ch:, :]
                    p["power"] = both[:ch, :]
                else:
                    p["pm"] = p["pm"] + p["power"] + _dot(p["pm"].astype(BF16), p16)
        for p in probs:
            h, gj = p["h"], p["gj"]
            p["eg"] = jnp.exp(gj)
            v = v_ref[p["rows"], h * HEAD_DIM:(h + 1) * HEAD_DIM].astype(F32)
            rhs = jnp.concatenate([v * p["bj"], p["kbeta"] * p["eg"]], axis=1)
            p["uw"] = rhs + _dot(p["pm"].astype(BF16), rhs.astype(BF16))
        for p in probs:
            j, n, gj = p["h"], p["n"], p["gj"]
            g_last = gj[ch - 1:ch, :]
            k_dec = p["k"] * jnp.exp(g_last - gj)
            lhs = jnp.concatenate([k_dec.T.astype(BF16), p["qk"].astype(BF16)], axis=0)
            big = _dot(lhs, p["uw"].astype(BF16))
            inject_s[j, n] = big[:HEAD_DIM, :HEAD_DIM]
            mix_s[j, n] = big[:HEAD_DIM, HEAD_DIM:].astype(BF16)
            local_s[j, p["rows"], :] = big[HEAD_DIM:, :HEAD_DIM]
            read_s[j, p["rows"], :] = (p["q"] * p["eg"] - big[HEAD_DIM:, HEAD_DIM:]).astype(BF16)
            keep_s[j, n] = jnp.broadcast_to(jnp.exp(g_last), (1, HEAD_DIM))
        return carry

    lax.fori_loop(0, n_chunks // group, local_terms, 0)

    def advance(n, states):
        nxt = []
        for j in range(heads):
            s16 = states[j].astype(BF16)
            state_s[j, n] = s16
            nxt.append(keep_s[j, n] * states[j] - _dot(mix_s[j, n], s16) + inject_s[j, n])
        return tuple(nxt)

    zero = jnp.zeros((HEAD_DIM, HEAD_DIM), F32)
    lax.fori_loop(0, n_chunks, advance, (zero,) * heads)

    def outputs(n, carry):
        rows = pl.ds(pl.multiple_of(n * ch, ch), ch)
        for j in range(heads):
            o = _dot(read_s[j, rows, :], state_s[j, n]) + local_s[j, rows, :]
            z = z_ref[rows, j * HEAD_DIM:(j + 1) * HEAD_DIM].astype(F32)
            inv = lax.rsqrt(jnp.mean(o * o, axis=1, keepdims=True) + RMS_EPS)
            o_ref[rows, j * HEAD_DIM:(j + 1) * HEAD_DIM] = (
                o * inv * nw * (z * jax.nn.sigmoid(z))).astype(o_ref.dtype)
        return carry

    lax.fori_loop(0, n_chunks, outputs, 0, unroll=4)


def _gdn_core(qkv, z, beta, gc, norm_w, batch, seq):
    hk, hv = GDN_K_HEADS, GDN_V_HEADS
    kh = GDN_KEY_HEADS_PER_STEP
    rep = kh * (hv // hk)
    narrow = kh * HEAD_DIM
    wide = rep * HEAD_DIM
    k_off = hk * HEAD_DIM // narrow
    v_off = 2 * hk * HEAD_DIM // wide
    n_chunks = seq // GDN_CHUNK
    return pl.pallas_call(
        functools.partial(_gdn_core_kernel, seq=seq),
        grid=(batch, hk // kh),
        in_specs=[pl.BlockSpec((seq, narrow), lambda b, h: (b, h)),
                  pl.BlockSpec((seq, narrow), lambda b, h: (b, k_off + h)),
                  pl.BlockSpec((seq, wide), lambda b, h: (b, v_off + h)),
                  pl.BlockSpec((seq, wide), lambda b, h: (b, h)),
                  pl.BlockSpec((seq, hv), lambda b, h: (b, 0)),
                  pl.BlockSpec((seq, hv), lambda b, h: (b, 0)),
                  pl.BlockSpec((1, HEAD_DIM), lambda b, h: (0, 0))],
        out_specs=pl.BlockSpec((seq, wide), lambda b, h: (b, h)),
        out_shape=jax.ShapeDtypeStruct((batch * seq, hv * HEAD_DIM), BF16),
        scratch_shapes=[pltpu.VMEM((rep, n_chunks, HEAD_DIM, HEAD_DIM), F32),
                        pltpu.VMEM((rep, n_chunks, HEAD_DIM, HEAD_DIM), BF16),
                        pltpu.VMEM((rep, seq, HEAD_DIM), F32),
                        pltpu.VMEM((rep, seq, HEAD_DIM), BF16),
                        pltpu.VMEM((rep, n_chunks, 1, HEAD_DIM), F32),
                        pltpu.VMEM((rep, n_chunks, HEAD_DIM, HEAD_DIM), BF16)],
        compiler_params=_params(2),
    )(qkv, qkv, qkv, z, beta, gc, norm_w.reshape(1, HEAD_DIM))


def _gated_deltanet(xb, w_in, conv_w, a_log, dt_bias, norm_w, w_o, j, batch, seq):
    key_dim = GDN_K_HEADS * HEAD_DIM
    val_dim = GDN_V_HEADS * HEAD_DIM
    conv_ch = 2 * key_dim + val_dim
    pre = _dense_matmul(xb, w_in, lead=(j,), col0=0, n_cols=conv_ch, out_dtype=BF16)
    z = _dense_matmul(xb, w_in, lead=(j,), col0=conv_ch, n_cols=val_dim, out_dtype=BF16)
    ba = _dense_matmul(xb, w_in[j, :, conv_ch + val_dim:])
    qkv = _gdn_conv(pre, conv_w, batch, seq)
    beta, gc = _gdn_gates(ba, a_log, dt_bias)
    og = _gdn_core(qkv, z, beta, gc, norm_w, batch, seq)
    return _dense_matmul(og, w_o, lead=(j,), tn=512)


def kernel(x, sb_w_qkv, sb_w_o, gdn_w_in, gdn_conv_w, gdn_a_log, gdn_dt_bias, gdn_norm_w, gdn_w_o, ln_mix_g, ln_mix_b, ln_ffn_g, ln_ffn_b, moe_router_w, moe_router_b, moe_w_gu, moe_b_gu, moe_w_down, moe_b_down):
    batch, seq, d = x.shape
    xf = x.reshape(batch * seq, d)
    xb = xf.astype(BF16)
    for i in range(DEPTH):
        j = i // 2
        if i % 2 == 0:
            qkv = _dense_matmul(xb, sb_w_qkv, lead=(j,), out_dtype=BF16)
            attn = _sb_attention(qkv, batch, seq)
            mix = _dense_matmul(attn, sb_w_o, lead=(j,))
        else:
            mix = _gated_deltanet(xb, gdn_w_in, gdn_conv_w[j], gdn_a_log[j], gdn_dt_bias[j],
                                  gdn_norm_w[j], gdn_w_o, j, batch, seq)
        x1, slabs, e_idx, gate, rank, counts = _ln_router(xf, mix, ln_mix_g[i], ln_mix_b[i],
                                                          moe_router_w[i], moe_router_b[i])
        last = i == DEPTH - 1
        outs = _moe_layer(x1, slabs, e_idx, gate, rank, counts, moe_w_gu, moe_b_gu[i], moe_w_down,
                          moe_b_down[i], i, ln_ffn_g[i], ln_ffn_b[i], with_bf16=not last)
        xf = outs[0]
        if not last:
            xb = outs[1]
    return xf.reshape(batch, seq, d)
```

```python
import functools

import jax
import jax.numpy as jnp
from jax import lax
from jax.experimental import pallas as pl
from jax.experimental.pallas import tpu as pltpu

F32, BF16, I32 = jnp.float32, jnp.bfloat16, jnp.int32

DEPTH = 2
N_EXPERTS = 32
TOP_K = 4
HEAD_DIM = 128
LANES = 128
SB_HEADS = 16
GDN_K_HEADS = 16
GDN_V_HEADS = 32
GDN_CONV = 4
GDN_CHUNK = 64
SWIGLU_LIMIT = 7.0
SWIGLU_ALPHA = 1.702
DEEPNORM_ALPHA = (2 * DEPTH) ** 0.25
LN_EPS = 1e-5
RMS_EPS = 1e-6
L2_EPS = 1e-6

VMEM_LIMIT_BYTES = 56 * 1024 * 1024
GDN_KEY_HEADS_PER_STEP = 2
GDN_GROUP = 8
CONV_HALO = 16
GATHER_AHEAD = 2
MOE_ROWS = 256
ATTN_BLOCK = 256
ATTN_SPAN = 4


def _params(n_axes):
    return pltpu.CompilerParams(dimension_semantics=("arbitrary",) * n_axes,
                                vmem_limit_bytes=VMEM_LIMIT_BYTES)


def _iota(shape, dim):
    return lax.broadcasted_iota(I32, shape, dim)


def _split_bf16(x):
    hi = x.astype(BF16)
    lo = (x - hi.astype(F32)).astype(BF16)
    return hi, lo


def _dot(a, b):
    return jnp.dot(a, b, preferred_element_type=F32)


def _dot_nt(a, b):
    return lax.dot_general(a, b, (((1,), (1,)), ((), ())), preferred_element_type=F32)


def _mm_kernel(a_ref, w_ref, o_ref, w_s):
    @pl.when(pl.program_id(1) == 0)
    def _():
        w_s[...] = w_ref[...].astype(BF16)

    o_ref[...] = _dot(a_ref[...], w_s[...]).astype(o_ref.dtype)


def _dense_matmul(a, w, *, lead=(), col0=0, n_cols=None, out_dtype=F32, tm=512, tn=1024):
    m_rows, k_dim = a.shape
    n_cols = w.shape[-1] if n_cols is None else n_cols
    tn = min(tn, n_cols)
    tm = min(tm, m_rows)
    assert n_cols % tn == 0 and col0 % tn == 0 and m_rows % tm == 0
    off = col0 // tn
    assert len(lead) == w.ndim - 2
    return pl.pallas_call(
        _mm_kernel,
        grid=(n_cols // tn, m_rows // tm),
        in_specs=[pl.BlockSpec((tm, k_dim), lambda n, m: (m, 0)),
                  pl.BlockSpec((None,) * len(lead) + (k_dim, tn), lambda n, m: lead + (0, n + off))],
        out_specs=pl.BlockSpec((tm, tn), lambda n, m: (m, n)),
        out_shape=jax.ShapeDtypeStruct((m_rows, n_cols), out_dtype),
        scratch_shapes=[pltpu.VMEM((k_dim, tn), BF16)],
        compiler_params=_params(2),
    )(a, w)


def _sb_attn_kernel(q_ref, k_ref, v_ref, o_ref, *, blk, span, scale):
    i = pl.program_id(2)
    q = q_ref[...]
    rows = _iota((blk, blk), 0)
    cols = _iota((blk, blk), 1)
    later = (rows > cols).astype(BF16)
    causal = cols < rows

    def visit(blocks, carry, acc, diagonal):
        n = len(blocks)
        ks = [k_ref[pl.ds(pl.multiple_of(j * blk, blk), blk), :] for j in blocks]
        vs = [v_ref[pl.ds(pl.multiple_of(j * blk, blk), blk), :] for j in blocks]
        zs = [_dot_nt(q, kb) * scale for kb in ks]
        log_beta, log_1m, parts = [], [], []
        for idx, z in enumerate(zs):
            lb = jnp.minimum(z, 0.0) - jnp.log(1.0 + jnp.exp(-jnp.abs(z)))
            lm = lb - z
            if diagonal and idx == 0:
                lm = jnp.where(causal, lm, 0.0)
            log_beta.append(lb)
            log_1m.append(lm)
            parts.extend(_split_bf16(lm))
        sums = _dot(jnp.concatenate(parts, axis=0), later)
        for idx in range(n):
            after = sums[2 * idx * blk:(2 * idx + 1) * blk] + sums[(2 * idx + 1) * blk:(2 * idx + 2) * blk]
            w = jnp.exp(log_beta[idx] + after + carry)
            if diagonal and idx == 0:
                w = jnp.where(causal, w, 0.0)
            acc = acc + _dot(w.astype(BF16), vs[idx])
            carry = carry + jnp.sum(log_1m[idx], axis=1, keepdims=True)
        return carry, acc

    start = (jnp.zeros((blk, 1), F32), jnp.zeros((blk, HEAD_DIM), F32))
    extra = i % span
    first = [functools.partial(visit, [i - s for s in range(n + 1)], *start, True) for n in range(span)]
    carry, acc = lax.switch(extra, first)

    def body(t, state):
        j = i - extra - 1 - span * t
        return visit([j - s for s in range(span)], *state, False)

    carry, acc = lax.fori_loop(0, (i - extra) // span, body, (carry, acc))
    o_ref[...] = acc.astype(o_ref.dtype)


def _sb_attention(qkv, batch, seq):
    blk = min(ATTN_BLOCK, seq)
    nq = seq // blk
    h = SB_HEADS
    kernel = functools.partial(_sb_attn_kernel, blk=blk, span=min(ATTN_SPAN, nq), scale=HEAD_DIM ** -0.5)
    return pl.pallas_call(
        kernel,
        grid=(batch, h, nq),
        in_specs=[pl.BlockSpec((blk, HEAD_DIM), lambda b, hh, i: (b * nq + i, hh)),
                  pl.BlockSpec((seq, HEAD_DIM), lambda b, hh, i: (b, h + hh)),
                  pl.BlockSpec((seq, HEAD_DIM), lambda b, hh, i: (b, 2 * h + hh))],
        out_specs=pl.BlockSpec((blk, HEAD_DIM), lambda b, hh, i: (b * nq + i, hh)),
        out_shape=jax.ShapeDtypeStruct((batch * seq, h * HEAD_DIM), BF16),
        compiler_params=_params(3),
    )(qkv, qkv, qkv)


def _deepnorm_ln(x, upd, g, b):
    h = DEEPNORM_ALPHA * x + upd
    mu = jnp.mean(h, axis=-1, keepdims=True)
    hc = h - mu
    var = jnp.mean(hc * hc, axis=-1, keepdims=True)
    return hc * lax.rsqrt(var + LN_EPS) * g + b


def _ln_router_kernel(x_ref, u_ref, g_ref, b_ref, rw_ref, rb_ref,
                      xo_ref, e_ref, gate_ref, rank_ref, cnt_ref, base_ref, *, tm):
    i = pl.program_id(0)

    @pl.when(i == 0)
    def _():
        base_ref[...] = jnp.zeros_like(base_ref)

    y = _deepnorm_ln(x_ref[...], u_ref[...], g_ref[...], b_ref[...])
    xo_ref[...] = y

    yh, yl = _split_bf16(y)
    wh, wl = _split_bf16(rw_ref[...])
    logits = _dot(yh, wh) + _dot(yl, wh) + _dot(yh, wl) + rb_ref[...]

    lane = _iota((tm, N_EXPERTS), 1)
    slot = _iota((tm, TOP_K), 1)
    work = logits
    picks, tops = [], []
    chosen = jnp.zeros((tm, N_EXPERTS), F32)
    for _ in range(TOP_K):
        top = jnp.max(work, axis=1, keepdims=True)
        idx = jnp.min(jnp.where(work == top, lane, N_EXPERTS), axis=1, keepdims=True)
        hit = lane == idx
        chosen = jnp.where(hit, 1.0, chosen)
        work = jnp.where(hit, -jnp.inf, work)
        picks.append(idx)
        tops.append(top)

    exps = [jnp.exp(t - tops[0]) for t in tops]
    denom = exps[0] + exps[1] + exps[2] + exps[3]

    before = (_iota((tm, tm), 0) > _iota((tm, tm), 1)).astype(BF16)
    ahead = base_ref[...] + _dot(before, chosen.astype(BF16))

    e_out = jnp.zeros((tm, TOP_K), I32)
    g_out = jnp.zeros((tm, TOP_K), F32)
    r_out = jnp.zeros((tm, TOP_K), F32)
    for k in range(TOP_K):
        rank_k = jnp.sum(jnp.where(lane == picks[k], ahead, 0.0), axis=1, keepdims=True)
        e_out = jnp.where(slot == k, picks[k], e_out)
        g_out = jnp.where(slot == k, exps[k] / denom, g_out)
        r_out = jnp.where(slot == k, rank_k, r_out)
    e_ref[...] = e_out
    gate_ref[...] = g_out
    rank_ref[...] = r_out.astype(I32)

    base_ref[...] = base_ref[...] + jnp.sum(chosen, axis=0, keepdims=True)
    cnt_ref[...] = base_ref[...]


def _ln_router(x, upd, g, b, router_w, router_b, tm=256):
    m_rows, d = x.shape
    tm = min(tm, m_rows)
    row = lambda i: (i, 0)
    fixed = lambda i: (0, 0)
    return pl.pallas_call(
        functools.partial(_ln_router_kernel, tm=tm),
        grid=(m_rows // tm,),
        in_specs=[pl.BlockSpec((tm, d), row), pl.BlockSpec((tm, d), row),
                  pl.BlockSpec((1, d), fixed), pl.BlockSpec((1, d), fixed),
                  pl.BlockSpec((d, N_EXPERTS), fixed), pl.BlockSpec((1, N_EXPERTS), fixed)],
        out_specs=[pl.BlockSpec((tm, d), row), pl.BlockSpec((tm, TOP_K), row),
                   pl.BlockSpec((tm, TOP_K), row), pl.BlockSpec((tm, TOP_K), row),
                   pl.BlockSpec((1, N_EXPERTS), fixed)],
        out_shape=[jax.ShapeDtypeStruct((m_rows, d), F32),
                   jax.ShapeDtypeStruct((m_rows, TOP_K), I32),
                   jax.ShapeDtypeStruct((m_rows, TOP_K), F32),
                   jax.ShapeDtypeStruct((m_rows, TOP_K), I32),
                   jax.ShapeDtypeStruct((1, N_EXPERTS), F32)],
        scratch_shapes=[pltpu.VMEM((1, N_EXPERTS), F32)],
        compiler_params=_params(1),
    )(x, upd, g.reshape(1, d), b.reshape(1, d), router_w, router_b.reshape(1, N_EXPERTS))


def _row_copy(src_hbm, row, dst, sem):
    return pltpu.make_async_copy(src_hbm.at[pl.ds(row, 1), :], dst, sem)


def _ring_starts(i, n_blocks, start):
    @pl.when(i == 0)
    def _():
        for b in range(GATHER_AHEAD):
            @pl.when(b < n_blocks)
            def _():
                start(b, b)

    nxt = i + GATHER_AHEAD
    for slot in range(GATHER_AHEAD + 1):
        @pl.when(jnp.logical_and(nxt < n_blocks, nxt % (GATHER_AHEAD + 1) == slot))
        def _():
            start(nxt, slot)


def _dispatch_kernel(tok_ref, meta_ref, x_hbm, o_ref, buf, sem, *, rb):
    i = pl.program_id(0)
    n_active = meta_ref[0]

    def start(blk, slot):
        for r in range(rb):
            _row_copy(x_hbm, tok_ref[blk * rb + r], buf.at[slot, pl.ds(r, 1), :], sem.at[slot]).start()

    _ring_starts(i, n_active, start)

    @pl.when(i < n_active)
    def _():
        slot = i % (GATHER_AHEAD + 1)
        pltpu.make_async_copy(x_hbm.at[pl.ds(0, rb), :], buf.at[slot], sem.at[slot]).wait()
        o_ref[...] = buf[slot].astype(BF16)

    @pl.when(i >= n_active)
    def _():
        o_ref[...] = jnp.zeros_like(o_ref)


def _dispatch(x, row_tok, meta, rb):
    m_rows, d = x.shape
    n_rows = row_tok.shape[0]
    slots = GATHER_AHEAD + 1
    return pl.pallas_call(
        functools.partial(_dispatch_kernel, rb=rb),
        grid_spec=pltpu.PrefetchScalarGridSpec(
            num_scalar_prefetch=2,
            grid=(n_rows // rb,),
            in_specs=[pl.BlockSpec(memory_space=pl.ANY)],
            out_specs=pl.BlockSpec((rb, d), lambda i, tok, meta: (i, 0)),
            scratch_shapes=[pltpu.VMEM((slots, rb, d), F32), pltpu.SemaphoreType.DMA((slots,))]),
        out_shape=jax.ShapeDtypeStruct((n_rows, d), BF16),
        compiler_params=_params(1),
    )(row_tok, meta, x)


def _expert_weights(be_ref, nxt_ref, gidx_ref, meta_ref, w_hbm, wbuf, sem, *, layer, w_cols):
    n = pl.program_id(0)
    m = pl.program_id(1)
    n_tiles = pl.num_programs(0)
    n_active, n_groups = meta_ref[0], meta_ref[1]
    active = m < n_active
    fresh = jnp.logical_and(active, jnp.logical_or(m == 0, be_ref[m] != be_ref[jnp.maximum(m - 1, 0)]))
    half = (n * n_groups + gidx_ref[m]) % 2

    def tile(e, col, h):
        cols = pl.ds(pl.multiple_of(col * w_cols, w_cols), w_cols)
        return pltpu.make_async_copy(w_hbm.at[layer, e, :, cols], wbuf.at[h], sem.at[h])

    @pl.when(jnp.logical_and(fresh, jnp.logical_and(n == 0, m == 0)))
    def _():
        tile(be_ref[0], 0, 0).start()

    @pl.when(fresh)
    def _():
        nxt_col = jnp.where(gidx_ref[m] == n_groups - 1, n + 1, n)

        @pl.when(nxt_col < n_tiles)
        def _():
            tile(nxt_ref[m], nxt_col, 1 - half).start()

        tile(be_ref[m], n, half).wait()

    return active, fresh, half


def _emit_rows(active, fill_ref, a_ref, o_ref, compute):
    rb = a_ref.shape[0]
    half_rows = rb // 2
    wide = fill_ref[pl.program_id(1)] > half_rows

    @pl.when(jnp.logical_and(active, wide))
    def _():
        o_ref[...] = compute(a_ref[...]).astype(o_ref.dtype)

    @pl.when(jnp.logical_and(active, jnp.logical_not(wide)))
    def _():
        o_ref[:half_rows, :] = compute(a_ref[:half_rows, :]).astype(o_ref.dtype)
        o_ref[half_rows:, :] = jnp.zeros((rb - half_rows, o_ref.shape[1]), o_ref.dtype)

    @pl.when(jnp.logical_not(active))
    def _():
        o_ref[...] = jnp.zeros_like(o_ref)


def _gate_up_kernel(be_ref, nxt_ref, gidx_ref, fill_ref, meta_ref, a_ref, w_hbm, bg_ref, bl_ref, o_ref,
                    wbuf, sem, wg_s, wl_s, *, tf, layer):
    active, fresh, half = _expert_weights(be_ref, nxt_ref, gidx_ref, meta_ref, w_hbm, wbuf, sem,
                                          layer=layer, w_cols=2 * tf)
    grp = 2 * HEAD_DIM

    @pl.when(fresh)
    def _():
        r = _iota((grp, grp), 0)
        c = _iota((grp, grp), 1)
        source = jnp.where(c < HEAD_DIM, 2 * c, 2 * (c - HEAD_DIM) + 1)
        pick = (r == source).astype(BF16)
        for g in range(2 * tf // grp):
            wb = wbuf[half, :, g * grp:(g + 1) * grp].astype(BF16)
            sel = _dot(wb, pick)
            wg_s[:, g * HEAD_DIM:(g + 1) * HEAD_DIM] = sel[:, :HEAD_DIM].astype(BF16)
            wl_s[:, g * HEAD_DIM:(g + 1) * HEAD_DIM] = sel[:, HEAD_DIM:].astype(BF16)

    def swiglu(a):
        glu = jnp.minimum(_dot(a, wg_s[...]) + bg_ref[0], SWIGLU_LIMIT)
        lin = jnp.clip(_dot(a, wl_s[...]) + bl_ref[0], -SWIGLU_LIMIT, SWIGLU_LIMIT)
        return glu * jax.nn.sigmoid(SWIGLU_ALPHA * glu) * (lin + 1.0)

    _emit_rows(active, fill_ref, a_ref, o_ref, swiglu)


def _down_kernel(be_ref, nxt_ref, gidx_ref, fill_ref, meta_ref, a_ref, w_hbm, b_ref, o_ref, wbuf, sem, w_s,
                 *, tn, layer):
    active, fresh, half = _expert_weights(be_ref, nxt_ref, gidx_ref, meta_ref, w_hbm, wbuf, sem,
                                          layer=layer, w_cols=tn)

    @pl.when(fresh)
    def _():
        w_s[...] = wbuf[half].astype(BF16)

    _emit_rows(active, fill_ref, a_ref, o_ref, lambda a: _dot(a, w_s[...]) + b_ref[0])


def _grouped_specs(rb, k_dim, b_cols, n_bias):
    def m_eff(m, meta):
        return jnp.minimum(m, meta[0] - 1)
    a_spec = pl.BlockSpec((rb, k_dim), lambda n, m, be, nxt, gidx, fill, meta: (m_eff(m, meta), 0))
    w_spec = pl.BlockSpec(memory_space=pl.ANY)
    b_spec = pl.BlockSpec((1, 1, b_cols), lambda n, m, be, nxt, gidx, fill, meta: (be[m_eff(m, meta)], 0, n))
    o_spec = pl.BlockSpec((rb, b_cols), lambda n, m, be, nxt, gidx, fill, meta: (m, n))
    return [a_spec, w_spec] + [b_spec] * n_bias, o_spec


def _moe_gate_up(xs, w_gu, layer, b_glu, b_lin, tables, rb, tf=1024):
    n_rows, d = xs.shape
    ff = w_gu.shape[3] // 2
    in_specs, o_spec = _grouped_specs(rb, d, tf, 2)
    return pl.pallas_call(
        functools.partial(_gate_up_kernel, tf=tf, layer=layer),
        grid_spec=pltpu.PrefetchScalarGridSpec(
            num_scalar_prefetch=5, grid=(ff // tf, n_rows // rb),
            in_specs=in_specs, out_specs=o_spec,
            scratch_shapes=[pltpu.VMEM((2, d, 2 * tf), F32), pltpu.SemaphoreType.DMA((2,)),
                            pltpu.VMEM((d, tf), BF16), pltpu.VMEM((d, tf), BF16)]),
        out_shape=jax.ShapeDtypeStruct((n_rows, ff), BF16),
        compiler_params=_params(2),
    )(*tables, xs, w_gu, b_glu, b_lin)


def _moe_down(act, w_down, layer, b_down, tables, rb, tn=2048):
    n_rows, ff = act.shape
    d = w_down.shape[3]
    in_specs, o_spec = _grouped_specs(rb, ff, tn, 1)
    return pl.pallas_call(
        functools.partial(_down_kernel, tn=tn, layer=layer),
        grid_spec=pltpu.PrefetchScalarGridSpec(
            num_scalar_prefetch=5, grid=(d // tn, n_rows // rb),
            in_specs=in_specs, out_specs=o_spec,
            scratch_shapes=[pltpu.VMEM((2, ff, tn), F32), pltpu.SemaphoreType.DMA((2,)),
                            pltpu.VMEM((ff, tn), BF16)]),
        out_shape=jax.ShapeDtypeStruct((n_rows, d), F32),
        compiler_params=_params(2),
    )(*tables, act, w_down, b_down)


def _combine_kernel(dest_ref, y_hbm, x_ref, gate_ref, g_ref, b_ref, *rest, tm, with_bf16):
    if with_bf16:
        xo_ref, xb_ref, buf, sem = rest
    else:
        xo_ref, buf, sem = rest
    i = pl.program_id(0)
    nb = pl.num_programs(0)
    n_slots = tm * TOP_K

    def start(blk, slot):
        for tok in range(tm):
            for k in range(TOP_K):
                row = dest_ref[blk * n_slots + tok * TOP_K + k]
                _row_copy(y_hbm, row, buf.at[slot, k, pl.ds(tok, 1), :], sem.at[slot]).start()

    _ring_starts(i, nb, start)

    slot = i % (GATHER_AHEAD + 1)
    for k in range(TOP_K):
        pltpu.make_async_copy(y_hbm.at[pl.ds(0, tm), :], buf.at[slot, k], sem.at[slot]).wait()
    gate = gate_ref[...]
    cur = buf.at[slot]
    ffn = gate[:, 0:1] * cur[0]
    for k in range(1, TOP_K):
        ffn = ffn + gate[:, k:k + 1] * cur[k]
    y = _deepnorm_ln(x_ref[...], ffn, g_ref[...], b_ref[...])
    xo_ref[...] = y
    if with_bf16:
        xb_ref[...] = y.astype(BF16)


def _combine(yb, dest, x, gate, g, b, with_bf16, tm=64):
    m_rows, d = x.shape
    tm = min(tm, m_rows)
    row = lambda i, dst: (i, 0)
    fixed = lambda i, dst: (0, 0)
    out_specs = [pl.BlockSpec((tm, d), row)]
    out_shape = [jax.ShapeDtypeStruct((m_rows, d), F32)]
    if with_bf16:
        out_specs.append(pl.BlockSpec((tm, d), row))
        out_shape.append(jax.ShapeDtypeStruct((m_rows, d), BF16))
    return pl.pallas_call(
        functools.partial(_combine_kernel, tm=tm, with_bf16=with_bf16),
        grid_spec=pltpu.PrefetchScalarGridSpec(
            num_scalar_prefetch=1, grid=(m_rows // tm,),
            in_specs=[pl.BlockSpec(memory_space=pl.ANY), pl.BlockSpec((tm, d), row),
                      pl.BlockSpec((tm, TOP_K), row), pl.BlockSpec((1, d), fixed), pl.BlockSpec((1, d), fixed)],
            out_specs=out_specs,
            scratch_shapes=[pltpu.VMEM((GATHER_AHEAD + 1, TOP_K, tm, d), F32),
                            pltpu.SemaphoreType.DMA((GATHER_AHEAD + 1,))]),
        out_shape=out_shape,
        compiler_params=_params(1),
    )(dest, yb, x, gate, g.reshape(1, d), b.reshape(1, d))


def _moe_layer(x1, e_idx, gate, rank, counts, w_gu, b_gu, w_down, b_down, layer, ln_g, ln_b, with_bf16):
    m_rows, d = x1.shape
    rb = MOE_ROWS
    n_slots = m_rows * TOP_K
    n_blocks = -(-n_slots // rb) + N_EXPERTS
    n_rows = n_blocks * rb
    cnt = counts.reshape(N_EXPERTS).astype(I32)
    blocks_e = (cnt + rb - 1) // rb
    blk_end = jnp.cumsum(blocks_e)
    row_start = (blk_end - blocks_e) * rb
    dest = (row_start[e_idx] + rank).reshape(n_slots)
    blk = jnp.arange(n_blocks, dtype=I32)
    block_e = jnp.minimum(jnp.sum((blk_end[None, :] <= blk[:, None]).astype(I32), axis=1), N_EXPERTS - 1)
    ids = jnp.arange(N_EXPERTS, dtype=I32)
    owns = blocks_e > 0
    group_of_e = jnp.cumsum(owns.astype(I32)) - 1
    later = jnp.where(jnp.logical_and(owns[None, :], ids[None, :] > ids[:, None]), ids[None, :], N_EXPERTS)
    first = jnp.min(jnp.where(owns, ids, N_EXPERTS))
    next_e = jnp.min(later, axis=1)
    next_e = jnp.where(next_e == N_EXPERTS, first, next_e)
    meta = jnp.stack([blk_end[-1], group_of_e[-1] + 1]).astype(I32)
    first_blk = blk_end - blocks_e
    fill = jnp.clip(cnt[block_e] - (blk - first_blk[block_e]) * rb, 0, rb).astype(I32)
    tables = (block_e, next_e[block_e].astype(I32), group_of_e[block_e].astype(I32), fill, meta)
    slot_tok = jnp.arange(n_slots, dtype=I32) // TOP_K
    row_tok = jnp.zeros((n_rows,), I32).at[dest].set(slot_tok)

    ff = w_down.shape[2]
    b_pairs = b_gu.reshape(N_EXPERTS, ff, 2)
    b_glu = b_pairs[:, :, 0].reshape(N_EXPERTS, 1, ff)
    b_lin = b_pairs[:, :, 1].reshape(N_EXPERTS, 1, ff)

    xs = _dispatch(x1, row_tok, meta, rb)
    act = _moe_gate_up(xs, w_gu, layer, b_glu, b_lin, tables, rb)
    yb = _moe_down(act, w_down, layer, b_down.reshape(N_EXPERTS, 1, d), tables, rb)
    return _combine(yb, dest, x1, gate, ln_g, ln_b, with_bf16)


def _conv_kernel(x_ref, halo_ref, w_ref, o_ref, ext, *, tt, nt, n_key_blocks, heads_per_block):
    i = pl.program_id(0)
    c = pl.program_id(1)
    first = (i % nt) == 0
    x = x_ref[...].astype(F32)
    ext[0:CONV_HALO, :] = jnp.where(first, 0.0, halo_ref[...].astype(F32))
    ext[CONV_HALO:, :] = x
    w = w_ref[...]
    acc = x * w[GDN_CONV - 1:GDN_CONV, :]
    for j in range(GDN_CONV - 1):
        acc = acc + ext[pl.ds(CONV_HALO - (GDN_CONV - 1) + j, tt), :] * w[j:j + 1, :]
    y = acc * jax.nn.sigmoid(acc)

    def normed(scale):
        for hh in range(heads_per_block):
            seg = y[:, hh * HEAD_DIM:(hh + 1) * HEAD_DIM]
            inv = lax.rsqrt(jnp.sum(seg * seg, axis=1, keepdims=True) + L2_EPS)
            o_ref[:, hh * HEAD_DIM:(hh + 1) * HEAD_DIM] = (seg * (inv * scale)).astype(o_ref.dtype)

    @pl.when(c < n_key_blocks)
    def _():
        normed(HEAD_DIM ** -0.5)

    @pl.when(jnp.logical_and(c >= n_key_blocks, c < 2 * n_key_blocks))
    def _():
        normed(1.0)

    @pl.when(c >= 2 * n_key_blocks)
    def _():
        o_ref[...] = y.astype(o_ref.dtype)


def _gdn_conv(pre, conv_w, batch, seq, tt=256, cb=1024):
    m_rows, ch = pre.shape
    tt = min(tt, seq)
    nt = seq // tt
    key_dim = GDN_K_HEADS * HEAD_DIM
    kernel = functools.partial(_conv_kernel, tt=tt, nt=nt, n_key_blocks=key_dim // cb,
                               heads_per_block=cb // HEAD_DIM)
    return pl.pallas_call(
        kernel,
        grid=(m_rows // tt, ch // cb),
        in_specs=[pl.BlockSpec((tt, cb), lambda i, c: (i, c)),
                  pl.BlockSpec((CONV_HALO, cb), lambda i, c: (jnp.maximum(i * (tt // CONV_HALO) - 1, 0), c)),
                  pl.BlockSpec((GDN_CONV, cb), lambda i, c: (0, c))],
        out_specs=pl.BlockSpec((tt, cb), lambda i, c: (i, c)),
        out_shape=jax.ShapeDtypeStruct((m_rows, ch), BF16),
        scratch_shapes=[pltpu.VMEM((tt + CONV_HALO, cb), F32)],
        compiler_params=_params(2),
    )(pre, pre, conv_w)


def _gates_kernel(ba_ref, alog_ref, dtb_ref, beta_ref, gc_ref, *, tm):
    ba = ba_ref[...]
    beta_ref[...] = jax.nn.sigmoid(ba[:, :GDN_V_HEADS])
    pre = ba[:, GDN_V_HEADS:] + dtb_ref[...]
    softplus = jnp.maximum(pre, 0.0) + jnp.log(1.0 + jnp.exp(-jnp.abs(pre)))
    g = -jnp.exp(alog_ref[...]) * softplus
    r = _iota((tm, tm), 0)
    c = _iota((tm, tm), 1)
    upto = jnp.logical_and(r >= c, r // GDN_CHUNK == c // GDN_CHUNK).astype(BF16)
    g1 = g.astype(BF16)
    rem = g - g1.astype(F32)
    g2 = rem.astype(BF16)
    g3 = (rem - g2.astype(F32)).astype(BF16)
    gc_ref[...] = _dot(upto, g1) + _dot(upto, g2) + _dot(upto, g3)


def _gdn_gates(ba, a_log, dt_bias, tm=256):
    m_rows = ba.shape[0]
    tm = min(tm, m_rows)
    hv = GDN_V_HEADS
    return pl.pallas_call(
        functools.partial(_gates_kernel, tm=tm),
        grid=(m_rows // tm,),
        in_specs=[pl.BlockSpec((tm, 2 * hv), lambda i: (i, 0)),
                  pl.BlockSpec((1, hv), lambda i: (0, 0)), pl.BlockSpec((1, hv), lambda i: (0, 0))],
        out_specs=[pl.BlockSpec((tm, hv), lambda i: (i, 0)), pl.BlockSpec((tm, hv), lambda i: (i, 0))],
        out_shape=[jax.ShapeDtypeStruct((m_rows, hv), F32), jax.ShapeDtypeStruct((m_rows, hv), F32)],
        compiler_params=_params(1),
    )(ba, a_log.reshape(1, hv), dt_bias.reshape(1, hv))


def _gdn_core_kernel(q_ref, k_ref, v_ref, z_ref, beta_ref, gc_ref, nw_ref, o_ref,
                     inject_s, mix_s, local_s, read_s, keep_s, state_s, *, seq):
    pair = pl.program_id(1)
    ch = GDN_CHUNK
    n_chunks = seq // ch
    rep = GDN_V_HEADS // GDN_K_HEADS
    kh = q_ref.shape[1] // HEAD_DIM
    heads = kh * rep
    ri = _iota((ch, ch), 0)
    ci = _iota((ch, ch), 1)
    incl, strict, eye = ri >= ci, ri > ci, ri == ci
    head_lane = _iota((ch, GDN_V_HEADS), 1)
    nw = nw_ref[...]

    group = max(1, min(GDN_GROUP // kh, n_chunks))

    def local_terms(t, carry):
        probs = []
        prods = []
        for c, kk in [(c, kk) for c in range(group) for kk in range(kh)]:
            n = t * group + c
            rows = pl.ds(pl.multiple_of(n * ch, ch), ch)
            q = q_ref[rows, kk * HEAD_DIM:(kk + 1) * HEAD_DIM].astype(F32)
            k = k_ref[rows, kk * HEAD_DIM:(kk + 1) * HEAD_DIM].astype(F32)
            beta_all = beta_ref[rows, :]
            gc_all = gc_ref[rows, :]
            kbetas = []
            for j in range(rep):
                h = kk * rep + j
                pick = head_lane == (heads * pair + h)
                bj = jnp.sum(jnp.where(pick, beta_all, 0.0), axis=1, keepdims=True)
                gj = jnp.sum(jnp.where(pick, gc_all, 0.0), axis=1, keepdims=True)
                grow = jnp.sum(jnp.where(eye, gj, 0.0), axis=0, keepdims=True)
                dj = jnp.where(incl, jnp.exp(jnp.where(incl, gj - grow, 0.0)), 0.0)
                kbetas.append(k * bj)
                probs.append(dict(c=len(prods), j=j, h=h, n=n, rows=rows, q=q, k=k, bj=bj, gj=gj, dj=dj,
                                  kbeta=kbetas[-1]))
            stacked = jnp.concatenate([kb.astype(BF16) for kb in kbetas] + [q.astype(BF16)], axis=0)
            prods.append(_dot_nt(stacked, k.astype(BF16)))
        for p in probs:
            pr = prods[p["c"]]
            j = p["j"]
            p["pm"] = -jnp.where(strict, pr[j * ch:(j + 1) * ch, :] * p["dj"], 0.0)
            p["qk"] = jnp.where(incl, pr[rep * ch:, :] * p["dj"], 0.0)
        for p in probs:
            n16 = p["pm"].astype(BF16)
            p["power"] = _dot(n16, n16)
        for step in range(5):
            for p in probs:
                p16 = p["power"].astype(BF16)
                if step < 4:
                    both = _dot(jnp.concatenate([p16, p["pm"].astype(BF16)], axis=0), p16)
                    p["pm"] = p["pm"] + p["power"] + both[ch:, :]
                    p["power"] = both[:ch, :]
                else:
                    p["pm"] = p["pm"] + p["power"] + _dot(p["pm"].astype(BF16), p16)
        for p in probs:
            h, gj = p["h"], p["gj"]
            p["eg"] = jnp.exp(gj)
            v = v_ref[p["rows"], h * HEAD_DIM:(h + 1) * HEAD_DIM].astype(F32)
            rhs = jnp.concatenate([v * p["bj"], p["kbeta"] * p["eg"]], axis=1)
            p["uw"] = rhs + _dot(p["pm"].astype(BF16), rhs.astype(BF16))
        for p in probs:
            j, n, gj = p["h"], p["n"], p["gj"]
            g_last = gj[ch - 1:ch, :]
            k_dec = p["k"] * jnp.exp(g_last - gj)
            lhs = jnp.concatenate([k_dec.T.astype(BF16), p["qk"].astype(BF16)], axis=0)
            big = _dot(lhs, p["uw"].astype(BF16))
            inject_s[j, n] = big[:HEAD_DIM, :HEAD_DIM]
            mix_s[j, n] = big[:HEAD_DIM, HEAD_DIM:].astype(BF16)
            local_s[j, p["rows"], :] = big[HEAD_DIM:, :HEAD_DIM]
            read_s[j, p["rows"], :] = (p["q"] * p["eg"] - big[HEAD_DIM:, HEAD_DIM:]).astype(BF16)
            keep_s[j, n] = jnp.broadcast_to(jnp.exp(g_last), (1, HEAD_DIM))
        return carry

    lax.fori_loop(0, n_chunks // group, local_terms, 0)

    def advance(n, states):
        nxt = []
        for j in range(heads):
            s16 = states[j].astype(BF16)
            state_s[j, n] = s16
            nxt.append(keep_s[j, n] * states[j] - _dot(mix_s[j, n], s16) + inject_s[j, n])
        return tuple(nxt)

    zero = jnp.zeros((HEAD_DIM, HEAD_DIM), F32)
    lax.fori_loop(0, n_chunks, advance, (zero,) * heads)

    def outputs(n, carry):
        rows = pl.ds(pl.multiple_of(n * ch, ch), ch)
        for j in range(heads):
            o = _dot(read_s[j, rows, :], state_s[j, n]) + local_s[j, rows, :]
            z = z_ref[rows, j * HEAD_DIM:(j + 1) * HEAD_DIM].astype(F32)
            inv = lax.rsqrt(jnp.mean(o * o, axis=1, keepdims=True) + RMS_EPS)
            o_ref[rows, j * HEAD_DIM:(j + 1) * HEAD_DIM] = (
                o * inv * nw * (z * jax.nn.sigmoid(z))).astype(o_ref.dtype)
        return carry

    lax.fori_loop(0, n_chunks, outputs, 0, unroll=4)


def _gdn_core(qkv, z, beta, gc, norm_w, batch, seq):
    hk, hv = GDN_K_HEADS, GDN_V_HEADS
    kh = GDN_KEY_HEADS_PER_STEP
    rep = kh * (hv // hk)
    narrow = kh * HEAD_DIM
    wide = rep * HEAD_DIM
    k_off = hk * HEAD_DIM // narrow
    v_off = 2 * hk * HEAD_DIM // wide
    n_chunks = seq // GDN_CHUNK
    return pl.pallas_call(
        functools.partial(_gdn_core_kernel, seq=seq),
        grid=(batch, hk // kh),
        in_specs=[pl.BlockSpec((seq, narrow), lambda b, h: (b, h)),
                  pl.BlockSpec((seq, narrow), lambda b, h: (b, k_off + h)),
                  pl.BlockSpec((seq, wide), lambda b, h: (b, v_off + h)),
                  pl.BlockSpec((seq, wide), lambda b, h: (b, h)),
                  pl.BlockSpec((seq, hv), lambda b, h: (b, 0)),
                  pl.BlockSpec((seq, hv), lambda b, h: (b, 0)),
                  pl.BlockSpec((1, HEAD_DIM), lambda b, h: (0, 0))],
        out_specs=pl.BlockSpec((seq, wide), lambda b, h: (b, h)),
        out_shape=jax.ShapeDtypeStruct((batch * seq, hv * HEAD_DIM), BF16),
        scratch_shapes=[pltpu.VMEM((rep, n_chunks, HEAD_DIM, HEAD_DIM), F32),
                        pltpu.VMEM((rep, n_chunks, HEAD_DIM, HEAD_DIM), BF16),
                        pltpu.VMEM((rep, seq, HEAD_DIM), F32),
                        pltpu.VMEM((rep, seq, HEAD_DIM), BF16),
                        pltpu.VMEM((rep, n_chunks, 1, HEAD_DIM), F32),
                        pltpu.VMEM((rep, n_chunks, HEAD_DIM, HEAD_DIM), BF16)],
        compiler_params=_params(2),
    )(qkv, qkv, qkv, z, beta, gc, norm_w.reshape(1, HEAD_DIM))


def _gated_deltanet(xb, w_in, conv_w, a_log, dt_bias, norm_w, w_o, j, batch, seq):
    key_dim = GDN_K_HEADS * HEAD_DIM
    val_dim = GDN_V_HEADS * HEAD_DIM
    conv_ch = 2 * key_dim + val_dim
    pre = _dense_matmul(xb, w_in, lead=(j,), col0=0, n_cols=conv_ch, out_dtype=BF16)
    z = _dense_matmul(xb, w_in, lead=(j,), col0=conv_ch, n_cols=val_dim, out_dtype=BF16)
    ba = _dense_matmul(xb, w_in[j, :, conv_ch + val_dim:])
    qkv = _gdn_conv(pre, conv_w, batch, seq)
    beta, gc = _gdn_gates(ba, a_log, dt_bias)
    og = _gdn_core(qkv, z, beta, gc, norm_w, batch, seq)
    return _dense_matmul(og, w_o, lead=(j,), tn=512)


def kernel(x, sb_w_qkv, sb_w_o, gdn_w_in, gdn_conv_w, gdn_a_log, gdn_dt_bias, gdn_norm_w, gdn_w_o, ln_mix_g, ln_mix_b, ln_ffn_g, ln_ffn_b, moe_router_w, moe_router_b, moe_w_gu, moe_b_gu, moe_w_down, moe_b_down):
    batch, seq, d = x.shape
    xf = x.reshape(batch * seq, d)
    xb = xf.astype(BF16)
    for i in range(DEPTH):
        j = i // 2
        if i % 2 == 0:
            qkv = _dense_matmul(xb, sb_w_qkv, lead=(j,), out_dtype=BF16)
            attn = _sb_attention(qkv, batch, seq)
            mix = _dense_matmul(attn, sb_w_o, lead=(j,))
        else:
            mix = _gated_deltanet(xb, gdn_w_in, gdn_conv_w[j], gdn_a_log[j], gdn_dt_bias[j],
                                  gdn_norm_w[j], gdn_w_o, j, batch, seq)
        x1, e_idx, gate, rank, counts = _ln_router(xf, mix, ln_mix_g[i], ln_mix_b[i],
                                                   moe_router_w[i], moe_router_b[i])
        last = i == DEPTH - 1
        outs = _moe_layer(x1, e_idx, gate, rank, counts, moe_w_gu, moe_b_gu[i], moe_w_down,
                          moe_b_down[i], i, ln_ffn_g[i], ln_ffn_b[i], with_bf16=not last)
        xf = outs[0]
        if not last:
            xb = outs[1]
    return xf.reshape(batch, seq, d)
```

```python
import functools

import jax
import jax.numpy as jnp
from jax import lax
from jax.experimental import pallas as pl
from jax.experimental.pallas import tpu as pltpu

F32, BF16, I32 = jnp.float32, jnp.bfloat16, jnp.int32

DEPTH = 2
N_EXPERTS = 32
TOP_K = 4
HEAD_DIM = 128
LANES = 128
SB_HEADS = 16
GDN_K_HEADS = 16
GDN_V_HEADS = 32
GDN_CONV = 4
GDN_CHUNK = 64
SWIGLU_LIMIT = 7.0
SWIGLU_ALPHA = 1.702
DEEPNORM_ALPHA = (2 * DEPTH) ** 0.25
LN_EPS = 1e-5
RMS_EPS = 1e-6
L2_EPS = 1e-6

VMEM_LIMIT_BYTES = 56 * 1024 * 1024
GDN_KEY_HEADS_PER_STEP = 2
GDN_GROUP = 8
CONV_HALO = 16
GATHER_AHEAD = 2
MOE_ROWS = 256
ATTN_BLOCK = 256
ATTN_SPAN = 4


def _params(n_axes):
    return pltpu.CompilerParams(dimension_semantics=("arbitrary",) * n_axes,
                                vmem_limit_bytes=VMEM_LIMIT_BYTES)


def _iota(shape, dim):
    return lax.broadcasted_iota(I32, shape, dim)


def _split_bf16(x):
    hi = x.astype(BF16)
    lo = (x - hi.astype(F32)).astype(BF16)
    return hi, lo


def _dot(a, b):
    return jnp.dot(a, b, preferred_element_type=F32)


def _dot_nt(a, b):
    return lax.dot_general(a, b, (((1,), (1,)), ((), ())), preferred_element_type=F32)


def _mm_kernel(a_ref, w_ref, o_ref, w_s, *, scaled_tiles, scale):
    @pl.when(pl.program_id(1) == 0)
    def _():
        w_s[...] = w_ref[...].astype(BF16)

    out = _dot(a_ref[...], w_s[...])
    if scaled_tiles:
        out = out * jnp.where(pl.program_id(0) < scaled_tiles, scale, 1.0)
    o_ref[...] = out.astype(o_ref.dtype)


def _dense_matmul(a, w, *, lead=(), col0=0, n_cols=None, out_dtype=F32, tm=512, tn=1024,
                  scaled_cols=0, scale=1.0):
    m_rows, k_dim = a.shape
    n_cols = w.shape[-1] if n_cols is None else n_cols
    tn = min(tn, n_cols)
    tm = min(tm, m_rows)
    assert n_cols % tn == 0 and col0 % tn == 0 and m_rows % tm == 0
    off = col0 // tn
    assert len(lead) == w.ndim - 2 and scaled_cols % tn == 0
    return pl.pallas_call(
        functools.partial(_mm_kernel, scaled_tiles=scaled_cols // tn, scale=scale),
        grid=(n_cols // tn, m_rows // tm),
        in_specs=[pl.BlockSpec((tm, k_dim), lambda n, m: (m, 0)),
                  pl.BlockSpec((None,) * len(lead) + (k_dim, tn), lambda n, m: lead + (0, n + off))],
        out_specs=pl.BlockSpec((tm, tn), lambda n, m: (m, n)),
        out_shape=jax.ShapeDtypeStruct((m_rows, n_cols), out_dtype),
        scratch_shapes=[pltpu.VMEM((k_dim, tn), BF16)],
        compiler_params=_params(2),
    )(a, w)


def _sb_attn_kernel(q_ref, k_ref, v_ref, o_ref, *, blk, span):
    i = pl.program_id(2)
    q = q_ref[...]
    rows = _iota((blk, blk), 0)
    cols = _iota((blk, blk), 1)
    later = (rows > cols).astype(BF16)
    causal = cols < rows

    def visit(blocks, carry, acc, diagonal):
        n = len(blocks)
        ks = [k_ref[pl.ds(pl.multiple_of(j * blk, blk), blk), :] for j in blocks]
        vs = [v_ref[pl.ds(pl.multiple_of(j * blk, blk), blk), :] for j in blocks]
        zs = [_dot_nt(q, kb) for kb in ks]
        log_beta, log_1m, parts = [], [], []
        for idx, z in enumerate(zs):
            lb = jnp.minimum(z, 0.0) - jnp.log(1.0 + jnp.exp(-jnp.abs(z)))
            lm = lb - z
            if diagonal and idx == 0:
                lm = jnp.where(causal, lm, 0.0)
            log_beta.append(lb)
            log_1m.append(lm)
            parts.extend(_split_bf16(lm))
        sums = _dot(jnp.concatenate(parts, axis=0), later)
        for idx in range(n):
            after = sums[2 * idx * blk:(2 * idx + 1) * blk] + sums[(2 * idx + 1) * blk:(2 * idx + 2) * blk]
            w = jnp.exp(log_beta[idx] + after + carry)
            if diagonal and idx == 0:
                w = jnp.where(causal, w, 0.0)
            acc = acc + _dot(w.astype(BF16), vs[idx])
            carry = carry + jnp.sum(log_1m[idx], axis=1, keepdims=True)
        return carry, acc

    start = (jnp.zeros((blk, 1), F32), jnp.zeros((blk, HEAD_DIM), F32))
    extra = i % span
    first = [functools.partial(visit, [i - s for s in range(n + 1)], *start, True) for n in range(span)]
    carry, acc = lax.switch(extra, first)

    def body(t, state):
        j = i - extra - 1 - span * t
        return visit([j - s for s in range(span)], *state, False)

    carry, acc = lax.fori_loop(0, (i - extra) // span, body, (carry, acc))
    o_ref[...] = acc.astype(o_ref.dtype)


def _sb_attention(qkv, batch, seq):
    blk = min(ATTN_BLOCK, seq)
    nq = seq // blk
    h = SB_HEADS
    kernel = functools.partial(_sb_attn_kernel, blk=blk, span=min(ATTN_SPAN, nq))
    return pl.pallas_call(
        kernel,
        grid=(batch, h, nq),
        in_specs=[pl.BlockSpec((blk, HEAD_DIM), lambda b, hh, i: (b * nq + i, hh)),
                  pl.BlockSpec((seq, HEAD_DIM), lambda b, hh, i: (b, h + hh)),
                  pl.BlockSpec((seq, HEAD_DIM), lambda b, hh, i: (b, 2 * h + hh))],
        out_specs=pl.BlockSpec((blk, HEAD_DIM), lambda b, hh, i: (b * nq + i, hh)),
        out_shape=jax.ShapeDtypeStruct((batch * seq, h * HEAD_DIM), BF16),
        compiler_params=_params(3),
    )(qkv, qkv, qkv)


def _deepnorm_ln(x, upd, g, b):
    h = DEEPNORM_ALPHA * x + upd
    mu = jnp.mean(h, axis=-1, keepdims=True)
    hc = h - mu
    var = jnp.mean(hc * hc, axis=-1, keepdims=True)
    return hc * lax.rsqrt(var + LN_EPS) * g + b


def _ln_router_kernel(x_ref, u_ref, g_ref, b_ref, rw_ref, rb_ref,
                      xo_ref, e_ref, gate_ref, rank_ref, cnt_ref, base_ref, *, tm):
    i = pl.program_id(0)

    @pl.when(i == 0)
    def _():
        base_ref[...] = jnp.zeros_like(base_ref)

    y = _deepnorm_ln(x_ref[...], u_ref[...], g_ref[...], b_ref[...])
    xo_ref[...] = y

    yh, yl = _split_bf16(y)
    wh, wl = _split_bf16(rw_ref[...])
    logits = _dot(yh, wh) + _dot(yl, wh) + _dot(yh, wl) + rb_ref[...]

    lane = _iota((tm, N_EXPERTS), 1)
    slot = _iota((tm, TOP_K), 1)
    work = logits
    picks, tops = [], []
    chosen = jnp.zeros((tm, N_EXPERTS), F32)
    for _ in range(TOP_K):
        top = jnp.max(work, axis=1, keepdims=True)
        idx = jnp.min(jnp.where(work == top, lane, N_EXPERTS), axis=1, keepdims=True)
        hit = lane == idx
        chosen = jnp.where(hit, 1.0, chosen)
        work = jnp.where(hit, -jnp.inf, work)
        picks.append(idx)
        tops.append(top)

    exps = [jnp.exp(t - tops[0]) for t in tops]
    denom = exps[0] + exps[1] + exps[2] + exps[3]

    before = (_iota((tm, tm), 0) > _iota((tm, tm), 1)).astype(BF16)
    ahead = base_ref[...] + _dot(before, chosen.astype(BF16))

    e_out = jnp.zeros((tm, TOP_K), I32)
    g_out = jnp.zeros((tm, TOP_K), F32)
    r_out = jnp.zeros((tm, TOP_K), F32)
    for k in range(TOP_K):
        rank_k = jnp.sum(jnp.where(lane == picks[k], ahead, 0.0), axis=1, keepdims=True)
        e_out = jnp.where(slot == k, picks[k], e_out)
        g_out = jnp.where(slot == k, exps[k] / denom, g_out)
        r_out = jnp.where(slot == k, rank_k, r_out)
    e_ref[...] = e_out
    gate_ref[...] = g_out
    rank_ref[...] = r_out.astype(I32)

    base_ref[...] = base_ref[...] + jnp.sum(chosen, axis=0, keepdims=True)
    cnt_ref[...] = base_ref[...]


def _ln_router(x, upd, g, b, router_w, router_b, tm=256):
    m_rows, d = x.shape
    tm = min(tm, m_rows)
    row = lambda i: (i, 0)
    fixed = lambda i: (0, 0)
    return pl.pallas_call(
        functools.partial(_ln_router_kernel, tm=tm),
        grid=(m_rows // tm,),
        in_specs=[pl.BlockSpec((tm, d), row), pl.BlockSpec((tm, d), row),
                  pl.BlockSpec((1, d), fixed), pl.BlockSpec((1, d), fixed),
                  pl.BlockSpec((d, N_EXPERTS), fixed), pl.BlockSpec((1, N_EXPERTS), fixed)],
        out_specs=[pl.BlockSpec((tm, d), row), pl.BlockSpec((tm, TOP_K), row),
                   pl.BlockSpec((tm, TOP_K), row), pl.BlockSpec((tm, TOP_K), row),
                   pl.BlockSpec((1, N_EXPERTS), fixed)],
        out_shape=[jax.ShapeDtypeStruct((m_rows, d), F32),
                   jax.ShapeDtypeStruct((m_rows, TOP_K), I32),
                   jax.ShapeDtypeStruct((m_rows, TOP_K), F32),
                   jax.ShapeDtypeStruct((m_rows, TOP_K), I32),
                   jax.ShapeDtypeStruct((1, N_EXPERTS), F32)],
        scratch_shapes=[pltpu.VMEM((1, N_EXPERTS), F32)],
        compiler_params=_params(1),
    )(x, upd, g.reshape(1, d), b.reshape(1, d), router_w, router_b.reshape(1, N_EXPERTS))


def _row_copy(src_hbm, row, dst, sem):
    return pltpu.make_async_copy(src_hbm.at[pl.ds(row, 1), :], dst, sem)


def _ring_starts(i, n_blocks, start):
    @pl.when(i == 0)
    def _():
        for b in range(GATHER_AHEAD):
            @pl.when(b < n_blocks)
            def _():
                start(b, b)

    nxt = i + GATHER_AHEAD
    for slot in range(GATHER_AHEAD + 1):
        @pl.when(jnp.logical_and(nxt < n_blocks, nxt % (GATHER_AHEAD + 1) == slot))
        def _():
            start(nxt, slot)


def _dispatch_kernel(tok_ref, meta_ref, x_hbm, o_ref, buf, sem, *, rb):
    i = pl.program_id(0)
    n_active = meta_ref[0]

    def start(blk, slot):
        for r in range(rb):
            _row_copy(x_hbm, tok_ref[blk * rb + r], buf.at[slot, pl.ds(r, 1), :], sem.at[slot]).start(
                priority=r % 2)

    _ring_starts(i, n_active, start)

    @pl.when(i < n_active)
    def _():
        slot = i % (GATHER_AHEAD + 1)
        pltpu.make_async_copy(x_hbm.at[pl.ds(0, rb), :], buf.at[slot], sem.at[slot]).wait()
        o_ref[...] = buf[slot].astype(BF16)

    @pl.when(i >= n_active)
    def _():
        o_ref[...] = jnp.zeros_like(o_ref)


def _dispatch(x, row_tok, meta, rb):
    m_rows, d = x.shape
    n_rows = row_tok.shape[0]
    slots = GATHER_AHEAD + 1
    return pl.pallas_call(
        functools.partial(_dispatch_kernel, rb=rb),
        grid_spec=pltpu.PrefetchScalarGridSpec(
            num_scalar_prefetch=2,
            grid=(n_rows // rb,),
            in_specs=[pl.BlockSpec(memory_space=pl.ANY)],
            out_specs=pl.BlockSpec((rb, d), lambda i, tok, meta: (i, 0)),
            scratch_shapes=[pltpu.VMEM((slots, rb, d), F32), pltpu.SemaphoreType.DMA((slots,))]),
        out_shape=jax.ShapeDtypeStruct((n_rows, d), BF16),
        compiler_params=_params(1),
    )(row_tok, meta, x)


def _expert_weights(be_ref, nxt_ref, gidx_ref, meta_ref, w_hbm, wbuf, sem, *, layer, w_cols):
    n = pl.program_id(0)
    m = pl.program_id(1)
    n_tiles = pl.num_programs(0)
    n_active, n_groups = meta_ref[0], meta_ref[1]
    active = m < n_active
    fresh = jnp.logical_and(active, jnp.logical_or(m == 0, be_ref[m] != be_ref[jnp.maximum(m - 1, 0)]))
    half = (n * n_groups + gidx_ref[m]) % 2

    def tile(e, col, h):
        cols = pl.ds(pl.multiple_of(col * w_cols, w_cols), w_cols)
        return pltpu.make_async_copy(w_hbm.at[layer, e, :, cols], wbuf.at[h], sem.at[h])

    @pl.when(jnp.logical_and(fresh, jnp.logical_and(n == 0, m == 0)))
    def _():
        tile(be_ref[0], 0, 0).start()

    @pl.when(fresh)
    def _():
        nxt_col = jnp.where(gidx_ref[m] == n_groups - 1, n + 1, n)

        @pl.when(nxt_col < n_tiles)
        def _():
            tile(nxt_ref[m], nxt_col, 1 - half).start()

        tile(be_ref[m], n, half).wait()

    return active, fresh, half


def _emit_rows(active, fill_ref, a_ref, o_ref, compute):
    rb = a_ref.shape[0]
    half_rows = rb // 2
    wide = fill_ref[pl.program_id(1)] > half_rows

    @pl.when(jnp.logical_and(active, wide))
    def _():
        o_ref[...] = compute(a_ref[...]).astype(o_ref.dtype)

    @pl.when(jnp.logical_and(active, jnp.logical_not(wide)))
    def _():
        o_ref[:half_rows, :] = compute(a_ref[:half_rows, :]).astype(o_ref.dtype)
        o_ref[half_rows:, :] = jnp.zeros((rb - half_rows, o_ref.shape[1]), o_ref.dtype)

    @pl.when(jnp.logical_not(active))
    def _():
        o_ref[...] = jnp.zeros_like(o_ref)


def _gate_up_kernel(be_ref, nxt_ref, gidx_ref, fill_ref, meta_ref, a_ref, w_hbm, bg_ref, bl_ref, o_ref,
                    wbuf, sem, wg_s, wl_s, *, tf, layer):
    active, fresh, half = _expert_weights(be_ref, nxt_ref, gidx_ref, meta_ref, w_hbm, wbuf, sem,
                                          layer=layer, w_cols=2 * tf)
    grp = 2 * HEAD_DIM

    @pl.when(fresh)
    def _():
        r = _iota((grp, grp), 0)
        c = _iota((grp, grp), 1)
        source = jnp.where(c < HEAD_DIM, 2 * c, 2 * (c - HEAD_DIM) + 1)
        pick = (r == source).astype(BF16)
        for g in range(2 * tf // grp):
            wb = wbuf[half, :, g * grp:(g + 1) * grp].astype(BF16)
            sel = _dot(wb, pick)
            wg_s[:, g * HEAD_DIM:(g + 1) * HEAD_DIM] = sel[:, :HEAD_DIM].astype(BF16)
            wl_s[:, g * HEAD_DIM:(g + 1) * HEAD_DIM] = sel[:, HEAD_DIM:].astype(BF16)

    def swiglu(a):
        glu = jnp.minimum(_dot(a, wg_s[...]) + bg_ref[0], SWIGLU_LIMIT)
        lin = jnp.clip(_dot(a, wl_s[...]) + bl_ref[0], -SWIGLU_LIMIT, SWIGLU_LIMIT)
        return glu * jax.nn.sigmoid(SWIGLU_ALPHA * glu) * (lin + 1.0)

    _emit_rows(active, fill_ref, a_ref, o_ref, swiglu)


def _down_kernel(be_ref, nxt_ref, gidx_ref, fill_ref, meta_ref, a_ref, w_hbm, b_ref, o_ref, wbuf, sem, w_s,
                 *, tn, layer):
    active, fresh, half = _expert_weights(be_ref, nxt_ref, gidx_ref, meta_ref, w_hbm, wbuf, sem,
                                          layer=layer, w_cols=tn)

    @pl.when(fresh)
    def _():
        w_s[...] = wbuf[half].astype(BF16)

    _emit_rows(active, fill_ref, a_ref, o_ref, lambda a: _dot(a, w_s[...]) + b_ref[0])


def _grouped_specs(rb, k_dim, b_cols, n_bias):
    def m_eff(m, meta):
        return jnp.minimum(m, meta[0] - 1)
    a_spec = pl.BlockSpec((rb, k_dim), lambda n, m, be, nxt, gidx, fill, meta: (m_eff(m, meta), 0))
    w_spec = pl.BlockSpec(memory_space=pl.ANY)
    b_spec = pl.BlockSpec((1, 1, b_cols), lambda n, m, be, nxt, gidx, fill, meta: (be[m_eff(m, meta)], 0, n))
    o_spec = pl.BlockSpec((rb, b_cols), lambda n, m, be, nxt, gidx, fill, meta: (m, n))
    return [a_spec, w_spec] + [b_spec] * n_bias, o_spec


def _moe_gate_up(xs, w_gu, layer, b_glu, b_lin, tables, rb, tf=1024):
    n_rows, d = xs.shape
    ff = w_gu.shape[3] // 2
    in_specs, o_spec = _grouped_specs(rb, d, tf, 2)
    return pl.pallas_call(
        functools.partial(_gate_up_kernel, tf=tf, layer=layer),
        grid_spec=pltpu.PrefetchScalarGridSpec(
            num_scalar_prefetch=5, grid=(ff // tf, n_rows // rb),
            in_specs=in_specs, out_specs=o_spec,
            scratch_shapes=[pltpu.VMEM((2, d, 2 * tf), F32), pltpu.SemaphoreType.DMA((2,)),
                            pltpu.VMEM((d, tf), BF16), pltpu.VMEM((d, tf), BF16)]),
        out_shape=jax.ShapeDtypeStruct((n_rows, ff), BF16),
        compiler_params=_params(2),
    )(*tables, xs, w_gu, b_glu, b_lin)


def _moe_down(act, w_down, layer, b_down, tables, rb, tn=2048):
    n_rows, ff = act.shape
    d = w_down.shape[3]
    in_specs, o_spec = _grouped_specs(rb, ff, tn, 1)
    return pl.pallas_call(
        functools.partial(_down_kernel, tn=tn, layer=layer),
        grid_spec=pltpu.PrefetchScalarGridSpec(
            num_scalar_prefetch=5, grid=(d // tn, n_rows // rb),
            in_specs=in_specs, out_specs=o_spec,
            scratch_shapes=[pltpu.VMEM((2, ff, tn), F32), pltpu.SemaphoreType.DMA((2,)),
                            pltpu.VMEM((ff, tn), BF16)]),
        out_shape=jax.ShapeDtypeStruct((n_rows, d), F32),
        compiler_params=_params(2),
    )(*tables, act, w_down, b_down)


def _combine_kernel(dest_ref, y_hbm, x_ref, gate_ref, g_ref, b_ref, *rest, tm, with_bf16):
    if with_bf16:
        xo_ref, xb_ref, buf, sem = rest
    else:
        xo_ref, buf, sem = rest
    i = pl.program_id(0)
    nb = pl.num_programs(0)
    n_slots = tm * TOP_K

    def start(blk, slot):
        for tok in range(tm):
            for k in range(TOP_K):
                row = dest_ref[blk * n_slots + tok * TOP_K + k]
                _row_copy(y_hbm, row, buf.at[slot, k, pl.ds(tok, 1), :], sem.at[slot]).start(
                    priority=k % 2)

    _ring_starts(i, nb, start)

    slot = i % (GATHER_AHEAD + 1)
    for k in range(TOP_K):
        pltpu.make_async_copy(y_hbm.at[pl.ds(0, tm), :], buf.at[slot, k], sem.at[slot]).wait()
    gate = gate_ref[...]
    cur = buf.at[slot]
    ffn = gate[:, 0:1] * cur[0]
    for k in range(1, TOP_K):
        ffn = ffn + gate[:, k:k + 1] * cur[k]
    y = _deepnorm_ln(x_ref[...], ffn, g_ref[...], b_ref[...])
    xo_ref[...] = y
    if with_bf16:
        xb_ref[...] = y.astype(BF16)


def _combine(yb, dest, x, gate, g, b, with_bf16, tm=128):
    m_rows, d = x.shape
    tm = min(tm, m_rows)
    row = lambda i, dst: (i, 0)
    fixed = lambda i, dst: (0, 0)
    out_specs = [pl.BlockSpec((tm, d), row)]
    out_shape = [jax.ShapeDtypeStruct((m_rows, d), F32)]
    if with_bf16:
        out_specs.append(pl.BlockSpec((tm, d), row))
        out_shape.append(jax.ShapeDtypeStruct((m_rows, d), BF16))
    return pl.pallas_call(
        functools.partial(_combine_kernel, tm=tm, with_bf16=with_bf16),
        grid_spec=pltpu.PrefetchScalarGridSpec(
            num_scalar_prefetch=1, grid=(m_rows // tm,),
            in_specs=[pl.BlockSpec(memory_space=pl.ANY), pl.BlockSpec((tm, d), row),
                      pl.BlockSpec((tm, TOP_K), row), pl.BlockSpec((1, d), fixed), pl.BlockSpec((1, d), fixed)],
            out_specs=out_specs,
            scratch_shapes=[pltpu.VMEM((GATHER_AHEAD + 1, TOP_K, tm, d), F32),
                            pltpu.SemaphoreType.DMA((GATHER_AHEAD + 1,))]),
        out_shape=out_shape,
        compiler_params=_params(1),
    )(dest, yb, x, gate, g.reshape(1, d), b.reshape(1, d))


def _moe_layer(x1, e_idx, gate, rank, counts, w_gu, b_gu, w_down, b_down, layer, ln_g, ln_b, with_bf16):
    m_rows, d = x1.shape
    rb = MOE_ROWS
    n_slots = m_rows * TOP_K
    n_blocks = -(-n_slots // rb) + N_EXPERTS
    n_rows = n_blocks * rb
    cnt = counts.reshape(N_EXPERTS).astype(I32)
    blocks_e = (cnt + rb - 1) // rb
    blk_end = jnp.cumsum(blocks_e)
    row_start = (blk_end - blocks_e) * rb
    dest = (row_start[e_idx] + rank).reshape(n_slots)
    blk = jnp.arange(n_blocks, dtype=I32)
    block_e = jnp.minimum(jnp.sum((blk_end[None, :] <= blk[:, None]).astype(I32), axis=1), N_EXPERTS - 1)
    ids = jnp.arange(N_EXPERTS, dtype=I32)
    owns = blocks_e > 0
    group_of_e = jnp.cumsum(owns.astype(I32)) - 1
    later = jnp.where(jnp.logical_and(owns[None, :], ids[None, :] > ids[:, None]), ids[None, :], N_EXPERTS)
    first = jnp.min(jnp.where(owns, ids, N_EXPERTS))
    next_e = jnp.min(later, axis=1)
    next_e = jnp.where(next_e == N_EXPERTS, first, next_e)
    meta = jnp.stack([blk_end[-1], group_of_e[-1] + 1]).astype(I32)
    first_blk = blk_end - blocks_e
    fill = jnp.clip(cnt[block_e] - (blk - first_blk[block_e]) * rb, 0, rb).astype(I32)
    tables = (block_e, next_e[block_e].astype(I32), group_of_e[block_e].astype(I32), fill, meta)
    slot_tok = jnp.arange(n_slots, dtype=I32) // TOP_K
    row_tok = jnp.zeros((n_rows,), I32).at[dest].set(slot_tok)

    ff = w_down.shape[2]
    b_pairs = b_gu.reshape(N_EXPERTS, ff, 2)
    b_glu = b_pairs[:, :, 0].reshape(N_EXPERTS, 1, ff)
    b_lin = b_pairs[:, :, 1].reshape(N_EXPERTS, 1, ff)

    xs = _dispatch(x1, row_tok, meta, rb)
    act = _moe_gate_up(xs, w_gu, layer, b_glu, b_lin, tables, rb)
    yb = _moe_down(act, w_down, layer, b_down.reshape(N_EXPERTS, 1, d), tables, rb)
    return _combine(yb, dest, x1, gate, ln_g, ln_b, with_bf16)


def _conv_kernel(x_ref, halo_ref, w_ref, o_ref, ext, *, tt, nt, n_key_blocks, heads_per_block):
    i = pl.program_id(0)
    c = pl.program_id(1)
    first = (i % nt) == 0
    x = x_ref[...].astype(F32)
    ext[0:CONV_HALO, :] = jnp.where(first, 0.0, halo_ref[...].astype(F32))
    ext[CONV_HALO:, :] = x
    w = w_ref[...]
    acc = x * w[GDN_CONV - 1:GDN_CONV, :]
    for j in range(GDN_CONV - 1):
        acc = acc + ext[pl.ds(CONV_HALO - (GDN_CONV - 1) + j, tt), :] * w[j:j + 1, :]
    y = acc * jax.nn.sigmoid(acc)

    def normed(scale):
        for hh in range(heads_per_block):
            seg = y[:, hh * HEAD_DIM:(hh + 1) * HEAD_DIM]
            inv = lax.rsqrt(jnp.sum(seg * seg, axis=1, keepdims=True) + L2_EPS)
            o_ref[:, hh * HEAD_DIM:(hh + 1) * HEAD_DIM] = (seg * (inv * scale)).astype(o_ref.dtype)

    @pl.when(c < n_key_blocks)
    def _():
        normed(HEAD_DIM ** -0.5)

    @pl.when(jnp.logical_and(c >= n_key_blocks, c < 2 * n_key_blocks))
    def _():
        normed(1.0)

    @pl.when(c >= 2 * n_key_blocks)
    def _():
        o_ref[...] = y.astype(o_ref.dtype)


def _gdn_conv(pre, conv_w, batch, seq, tt=256, cb=1024):
    m_rows, ch = pre.shape
    tt = min(tt, seq)
    nt = seq // tt
    key_dim = GDN_K_HEADS * HEAD_DIM
    kernel = functools.partial(_conv_kernel, tt=tt, nt=nt, n_key_blocks=key_dim // cb,
                               heads_per_block=cb // HEAD_DIM)
    return pl.pallas_call(
        kernel,
        grid=(m_rows // tt, ch // cb),
        in_specs=[pl.BlockSpec((tt, cb), lambda i, c: (i, c)),
                  pl.BlockSpec((CONV_HALO, cb), lambda i, c: (jnp.maximum(i * (tt // CONV_HALO) - 1, 0), c)),
                  pl.BlockSpec((GDN_CONV, cb), lambda i, c: (0, c))],
        out_specs=pl.BlockSpec((tt, cb), lambda i, c: (i, c)),
        out_shape=jax.ShapeDtypeStruct((m_rows, ch), BF16),
        scratch_shapes=[pltpu.VMEM((tt + CONV_HALO, cb), F32)],
        compiler_params=_params(2),
    )(pre, pre, conv_w)


def _gates_kernel(ba_ref, alog_ref, dtb_ref, beta_ref, gc_ref, *, tm):
    ba = ba_ref[...]
    beta_ref[...] = jax.nn.sigmoid(ba[:, :GDN_V_HEADS])
    pre = ba[:, GDN_V_HEADS:] + dtb_ref[...]
    softplus = jnp.maximum(pre, 0.0) + jnp.log(1.0 + jnp.exp(-jnp.abs(pre)))
    g = -jnp.exp(alog_ref[...]) * softplus
    r = _iota((tm, tm), 0)
    c = _iota((tm, tm), 1)
    upto = jnp.logical_and(r >= c, r // GDN_CHUNK == c // GDN_CHUNK).astype(BF16)
    g1 = g.astype(BF16)
    rem = g - g1.astype(F32)
    g2 = rem.astype(BF16)
    g3 = (rem - g2.astype(F32)).astype(BF16)
    gc_ref[...] = _dot(upto, g1) + _dot(upto, g2) + _dot(upto, g3)


def _gdn_gates(ba, a_log, dt_bias, tm=256):
    m_rows = ba.shape[0]
    tm = min(tm, m_rows)
    hv = GDN_V_HEADS
    return pl.pallas_call(
        functools.partial(_gates_kernel, tm=tm),
        grid=(m_rows // tm,),
        in_specs=[pl.BlockSpec((tm, 2 * hv), lambda i: (i, 0)),
                  pl.BlockSpec((1, hv), lambda i: (0, 0)), pl.BlockSpec((1, hv), lambda i: (0, 0))],
        out_specs=[pl.BlockSpec((tm, hv), lambda i: (i, 0)), pl.BlockSpec((tm, hv), lambda i: (i, 0))],
        out_shape=[jax.ShapeDtypeStruct((m_rows, hv), F32), jax.ShapeDtypeStruct((m_rows, hv), F32)],
        compiler_params=_params(1),
    )(ba, a_log.reshape(1, hv), dt_bias.reshape(1, hv))


def _gdn_core_kernel(q_ref, k_ref, v_ref, z_ref, beta_ref, gc_ref, nw_ref, o_ref,
                     inject_s, mix_s, local_s, read_s, keep_s, state_s, *, seq):
    pair = pl.program_id(1)
    ch = GDN_CHUNK
    n_chunks = seq // ch
    rep = GDN_V_HEADS // GDN_K_HEADS
    kh = q_ref.shape[1] // HEAD_DIM
    heads = kh * rep
    ri = _iota((ch, ch), 0)
    ci = _iota((ch, ch), 1)
    incl, strict, eye = ri >= ci, ri > ci, ri == ci
    head_lane = _iota((ch, GDN_V_HEADS), 1)
    nw = nw_ref[...]

    group = max(1, min(GDN_GROUP // kh, n_chunks))

    def local_terms(t, carry):
        probs = []
        prods = []
        for c, kk in [(c, kk) for c in range(group) for kk in range(kh)]:
            n = t * group + c
            rows = pl.ds(pl.multiple_of(n * ch, ch), ch)
            q = q_ref[rows, kk * HEAD_DIM:(kk + 1) * HEAD_DIM].astype(F32)
            k = k_ref[rows, kk * HEAD_DIM:(kk + 1) * HEAD_DIM].astype(F32)
            beta_all = beta_ref[rows, :]
            gc_all = gc_ref[rows, :]
            kbetas = []
            for j in range(rep):
                h = kk * rep + j
                pick = head_lane == (heads * pair + h)
                bj = jnp.sum(jnp.where(pick, beta_all, 0.0), axis=1, keepdims=True)
                gj = jnp.sum(jnp.where(pick, gc_all, 0.0), axis=1, keepdims=True)
                grow = jnp.sum(jnp.where(eye, gj, 0.0), axis=0, keepdims=True)
                dj = jnp.where(incl, jnp.exp(jnp.where(incl, gj - grow, 0.0)), 0.0)
                kbetas.append(k * bj)
                probs.append(dict(c=len(prods), j=j, h=h, n=n, rows=rows, q=q, k=k, bj=bj, gj=gj, dj=dj,
                                  kbeta=kbetas[-1]))
            stacked = jnp.concatenate([kb.astype(BF16) for kb in kbetas] + [q.astype(BF16)], axis=0)
            prods.append(_dot_nt(stacked, k.astype(BF16)))
        for p in probs:
            pr = prods[p["c"]]
            j = p["j"]
            p["pm"] = -jnp.where(strict, pr[j * ch:(j + 1) * ch, :] * p["dj"], 0.0)
            p["qk"] = jnp.where(incl, pr[rep * ch:, :] * p["dj"], 0.0)
        for p in probs:
            n16 = p["pm"].astype(BF16)
            p["power"] = _dot(n16, n16)
        for step in range(5):
            for p in probs:
                p16 = p["power"].astype(BF16)
                if step < 4:
                    both = _dot(jnp.concatenate([p16, p["pm"].astype(BF16)], axis=0), p16)
                    p["pm"] = p["pm"] + p["power"] + both[ch:, :]
                    p["power"] = both[:ch, :]
                else:
                    p["pm"] = p["pm"] + p["power"] + _dot(p["pm"].astype(BF16), p16)
        for p in probs:
            h, gj = p["h"], p["gj"]
            p["eg"] = jnp.exp(gj)
            v = v_ref[p["rows"], h * HEAD_DIM:(h + 1) * HEAD_DIM].astype(F32)
            rhs = jnp.concatenate([v * p["bj"], p["kbeta"] * p["eg"]], axis=1)
            p["uw"] = rhs + _dot(p["pm"].astype(BF16), rhs.astype(BF16))
        for p in probs:
            j, n, gj = p["h"], p["n"], p["gj"]
            g_last = gj[ch - 1:ch, :]
            k_dec = p["k"] * jnp.exp(g_last - gj)
            lhs = jnp.concatenate([k_dec.T.astype(BF16), p["qk"].astype(BF16)], axis=0)
            big = _dot(lhs, p["uw"].astype(BF16))
            inject_s[j, n] = big[:HEAD_DIM, :HEAD_DIM]
            mix_s[j, n] = big[:HEAD_DIM, HEAD_DIM:].astype(BF16)
            local_s[j, p["rows"], :] = big[HEAD_DIM:, :HEAD_DIM]
            read_s[j, p["rows"], :] = (p["q"] * p["eg"] - big[HEAD_DIM:, HEAD_DIM:]).astype(BF16)
            keep_s[j, n] = jnp.broadcast_to(jnp.exp(g_last), (1, HEAD_DIM))
        return carry

    lax.fori_loop(0, n_chunks // group, local_terms, 0)

    def advance(n, states):
        nxt = []
        for j in range(heads):
            s16 = states[j].astype(BF16)
            state_s[j, n] = s16
            nxt.append(keep_s[j, n] * states[j] - _dot(mix_s[j, n], s16) + inject_s[j, n])
        return tuple(nxt)

    zero = jnp.zeros((HEAD_DIM, HEAD_DIM), F32)
    lax.fori_loop(0, n_chunks, advance, (zero,) * heads)

    def outputs(n, carry):
        rows = pl.ds(pl.multiple_of(n * ch, ch), ch)
        for j in range(heads):
            o = _dot(read_s[j, rows, :], state_s[j, n]) + local_s[j, rows, :]
            z = z_ref[rows, j * HEAD_DIM:(j + 1) * HEAD_DIM].astype(F32)
            inv = lax.rsqrt(jnp.mean(o * o, axis=1, keepdims=True) + RMS_EPS)
            o_ref[rows, j * HEAD_DIM:(j + 1) * HEAD_DIM] = (
                o * inv * nw * (z * jax.nn.sigmoid(z))).astype(o_ref.dtype)
        return carry

    lax.fori_loop(0, n_chunks, outputs, 0, unroll=4)


def _gdn_core(qkv, z, beta, gc, norm_w, batch, seq):
    hk, hv = GDN_K_HEADS, GDN_V_HEADS
    kh = GDN_KEY_HEADS_PER_STEP
    rep = kh * (hv // hk)
    narrow = kh * HEAD_DIM
    wide = rep * HEAD_DIM
    k_off = hk * HEAD_DIM // narrow
    v_off = 2 * hk * HEAD_DIM // wide
    n_chunks = seq // GDN_CHUNK
    return pl.pallas_call(
        functools.partial(_gdn_core_kernel, seq=seq),
        grid=(batch, hk // kh),
        in_specs=[pl.BlockSpec((seq, narrow), lambda b, h: (b, h)),
                  pl.BlockSpec((seq, narrow), lambda b, h: (b, k_off + h)),
                  pl.BlockSpec((seq, wide), lambda b, h: (b, v_off + h)),
                  pl.BlockSpec((seq, wide), lambda b, h: (b, h)),
                  pl.BlockSpec((seq, hv), lambda b, h: (b, 0)),
                  pl.BlockSpec((seq, hv), lambda b, h: (b, 0)),
                  pl.BlockSpec((1, HEAD_DIM), lambda b, h: (0, 0))],
        out_specs=pl.BlockSpec((seq, wide), lambda b, h: (b, h)),
        out_shape=jax.ShapeDtypeStruct((batch * seq, hv * HEAD_DIM), BF16),
        scratch_shapes=[pltpu.VMEM((rep, n_chunks, HEAD_DIM, HEAD_DIM), F32),
                        pltpu.VMEM((rep, n_chunks, HEAD_DIM, HEAD_DIM), BF16),
                        pltpu.VMEM((rep, seq, HEAD_DIM), F32),
                        pltpu.VMEM((rep, seq, HEAD_DIM), BF16),
                        pltpu.VMEM((rep, n_chunks, 1, HEAD_DIM), F32),
                        pltpu.VMEM((rep, n_chunks, HEAD_DIM, HEAD_DIM), BF16)],
        compiler_params=_params(2),
    )(qkv, qkv, qkv, z, beta, gc, norm_w.reshape(1, HEAD_DIM))


def _gated_deltanet(xb, w_in, conv_w, a_log, dt_bias, norm_w, w_o, j, batch, seq):
    key_dim = GDN_K_HEADS * HEAD_DIM
    val_dim = GDN_V_HEADS * HEAD_DIM
    conv_ch = 2 * key_dim + val_dim
    pre = _dense_matmul(xb, w_in, lead=(j,), col0=0, n_cols=conv_ch, out_dtype=BF16)
    z = _dense_matmul(xb, w_in, lead=(j,), col0=conv_ch, n_cols=val_dim, out_dtype=BF16)
    ba = _dense_matmul(xb, w_in[j, :, conv_ch + val_dim:])
    qkv = _gdn_conv(pre, conv_w, batch, seq)
    beta, gc = _gdn_gates(ba, a_log, dt_bias)
    og = _gdn_core(qkv, z, beta, gc, norm_w, batch, seq)
    return _dense_matmul(og, w_o, lead=(j,), tn=512)


def kernel(x, sb_w_qkv, sb_w_o, gdn_w_in, gdn_conv_w, gdn_a_log, gdn_dt_bias, gdn_norm_w, gdn_w_o, ln_mix_g, ln_mix_b, ln_ffn_g, ln_ffn_b, moe_router_w, moe_router_b, moe_w_gu, moe_b_gu, moe_w_down, moe_b_down):
    batch, seq, d = x.shape
    xf = x.reshape(batch * seq, d)
    xb = xf.astype(BF16)
    for i in range(DEPTH):
        j = i // 2
        if i % 2 == 0:
            qkv = _dense_matmul(xb, sb_w_qkv, lead=(j,), out_dtype=BF16,
                                scaled_cols=SB_HEADS * HEAD_DIM, scale=HEAD_DIM ** -0.5)
            attn = _sb_attention(qkv, batch, seq)
            mix = _dense_matmul(attn, sb_w_o, lead=(j,))
        else:
            mix = _gated_deltanet(xb, gdn_w_in, gdn_conv_w[j], gdn_a_log[j], gdn_dt_bias[j],
                                  gdn_norm_w[j], gdn_w_o, j, batch, seq)
        x1, e_idx, gate, rank, counts = _ln_router(xf, mix, ln_mix_g[i], ln_mix_b[i],
                                                   moe_router_w[i], moe_router_b[i])
        last = i == DEPTH - 1
        outs = _moe_layer(x1, e_idx, gate, rank, counts, moe_w_gu, moe_b_gu[i], moe_w_down,
                          moe_b_down[i], i, ln_ffn_g[i], ln_ffn_b[i], with_bf16=not last)
        xf = outs[0]
        if not last:
            xb = outs[1]
    return xf.reshape(batch, seq, d)
```
